```python
import jax
import jax.numpy as jnp
from jax import lax
import numpy as np

D_MODEL = 1024
BATCH = 4
SEQ = 4096
DEPTH = 2
DEC_BATCH = 128
DEC_SEQ = 8
PAST_LEN = 16384
PAGE_SIZE = 128

GROUP_W = D_MODEL // 4
MIX_W = 4 * GROUP_W
A_HEADS = 4
A_HD = GROUP_W // A_HEADS
IDX_HEADS = 4
IDX_DIM = 64
TOPK_MAX = 256
B_W = GROUP_W
B_BLOCKS = 4
B_BD = B_W // B_BLOCKS
CONV_W = 4
LRU_C = 8.0
C_HEADS = 4
C_NOPE = 64
C_ROPE = 32
C_VD = GROUP_W // C_HEADS
C_Q_RANK = 192
C_KV_RANK = 128
ROPE_BASE = 10000.0
N_MEM = 256
M_HEADS = 4
M_HD = GROUP_W // M_HEADS
Q_BLOCK = 128
EPS = 1e-6

IN_SPLITS = (A_HEADS * A_HD, A_HD, A_HD, IDX_HEADS * IDX_DIM, IDX_DIM, IDX_HEADS, GROUP_W,
             B_W, B_W,
             C_Q_RANK, C_KV_RANK, C_ROPE, GROUP_W,
             M_HEADS * M_HD, GROUP_W)
IN_DIM = sum(IN_SPLITS)

kernel_name = 'hymba_dsa_rglru_mla_mem_step'


def rmsnorm(x, g):
    xf = x.astype(jnp.float32)
    y = xf * lax.rsqrt(jnp.mean(xf * xf, axis=-1, keepdims=True) + EPS)
    return (y * g.astype(jnp.float32)).astype(x.dtype)


def rope(x, pos):
    half = x.shape[-1] // 2
    freqs = ROPE_BASE ** (-jnp.arange(half, dtype=jnp.float32) / half)
    ang = pos[:, None] * freqs[None, :]
    shape = (ang.shape[0],) + (1,) * (x.ndim - 3) + (half,)
    cos = jnp.cos(ang).reshape(shape).astype(x.dtype)
    sin = jnp.sin(ang).reshape(shape).astype(x.dtype)
    x1, x2 = x[..., :half], x[..., half:]
    return jnp.concatenate([x1 * cos - x2 * sin, x1 * sin + x2 * cos], axis=-1)


def split_columns(u):
    parts, start = [], 0
    for w in IN_SPLITS:
        parts.append(u[..., start:start + w])
        start += w
    return parts


def mixer_inputs(x, p, pos):
    B_, T = x.shape[0], x.shape[1]
    h = rmsnorm(x, p['g_pre'])
    u = h @ p['w_in']
    (aq, ak, av, aqi, aki, awi, ag, bx, bg, cdq, cdkv, ckr, cg, mq, mg) = split_columns(u)
    cq = rmsnorm(cdq, p['g_q'])
    qf = (cq @ p['w_uq']).reshape(B_, T, C_HEADS, C_NOPE + C_ROPE)
    q_lat = jnp.einsum('bthd,rhd->bthr', qf[..., :C_NOPE], p['w_uk'])
    return {
        'aq': aq.reshape(B_, T, A_HEADS, A_HD), 'ak': ak, 'av': av,
        'aqi': aqi.reshape(B_, T, IDX_HEADS, IDX_DIM), 'aki': aki,
        'awi': awi * (IDX_HEADS ** -0.5), 'ag': ag,
        'bx': bx, 'bg': bg,
        'q_lat': q_lat, 'q_rope': rope(qf[..., C_NOPE:], pos),
        'c': rmsnorm(cdkv, p['g_kv']), 'kr': rope(ckr, pos), 'cg': cg,
        'mq': mq.reshape(B_, T, M_HEADS, M_HD), 'mg': mg,
    }


def indexer_topk(qi, ki, wi, mask, topk):
    s = jnp.einsum('...qhd,...sd->...qhs', qi, ki).astype(jnp.float32) * (IDX_DIM ** -0.5)
    score = jnp.einsum('...qhs,...qh->...qs', jax.nn.relu(s), wi.astype(jnp.float32))
    score = jnp.where(mask, score, -jnp.inf)
    val, idx = lax.top_k(score, topk)
    return idx, jnp.isfinite(val)


def sparse_attend(q, k_sel, v_sel, valid):
    s = jnp.einsum('...qhd,...qkd->...qhk', q, k_sel).astype(jnp.float32) * (A_HD ** -0.5)
    s = jnp.where(valid[..., :, None, :], s, -jnp.inf)
    pr = jax.nn.softmax(s, axis=-1).astype(v_sel.dtype)
    return jnp.einsum('...qhk,...qkd->...qhd', pr, v_sel)


def latent_attend(q_lat, q_rope, c, kr, mask):
    s = (jnp.einsum('...qhr,...sr->...qhs', q_lat, c)
         + jnp.einsum('...qhe,...se->...qhs', q_rope, kr)).astype(jnp.float32)
    s = jnp.where(mask[:, None, :], s * ((C_NOPE + C_ROPE) ** -0.5), -jnp.inf)
    pr = jax.nn.softmax(s, axis=-1).astype(c.dtype)
    return jnp.einsum('...qhs,...sr->...qhr', pr, c)


def mem_attend(q, mk, mv):
    s = jnp.einsum('bthd,bnhd->bthn', q, mk).astype(jnp.float32) * (M_HD ** -0.5)
    pr = jax.nn.softmax(s, axis=-1).astype(mv.dtype)
    return jnp.einsum('bthn,bnhd->bthd', pr, mv)


def rglru(xb, conv_buf, h0, p):
    B_, T = xb.shape[0], xb.shape[1]
    xp = jnp.concatenate([conv_buf.astype(xb.dtype), xb], axis=1)
    w = p['conv_w']
    y = p['conv_b'] + w[0] * xp[:, 0:T]
    for j in range(1, CONV_W):
        y = y + w[j] * xp[:, j:j + T]
    new_buf = xp[:, T:]
    yb = y.reshape(B_, T, B_BLOCKS, B_BD)
    r = jax.nn.sigmoid((jnp.einsum('btnd,nde->btne', yb, p['w_ra']).reshape(B_, T, B_W)
                        + p['b_ra']).astype(jnp.float32))
    i = jax.nn.sigmoid((jnp.einsum('btnd,nde->btne', yb, p['w_ri']).reshape(B_, T, B_W)
                        + p['b_ri']).astype(jnp.float32))
    log_a = -LRU_C * r * jax.nn.softplus(-p['lam'].astype(jnp.float32))
    a = jnp.exp(log_a)
    u = jnp.sqrt(-jnp.expm1(2.0 * log_a)) * i * y.astype(jnp.float32)

    def step(h, au):
        a_t, u_t = au
        h = a_t * h + u_t
        return h, h

    h_last, hs = lax.scan(step, h0.astype(jnp.float32),
                          (jnp.swapaxes(a, 0, 1), jnp.swapaxes(u, 0, 1)))
    return jnp.swapaxes(hs, 0, 1).astype(xb.dtype), new_buf, h_last.astype(xb.dtype)


def mixer_output(x, m, oa, ob, oc_lat, om, p):
    B_, T = x.shape[0], x.shape[1]
    oc = jnp.einsum('bthr,rhd->bthd', oc_lat, p['w_uv']).reshape(B_, T, C_HEADS * C_VD)
    cat = jnp.concatenate([
        oa.reshape(B_, T, GROUP_W) * jax.nn.silu(m['ag']),
        ob * jax.nn.silu(m['bg']),
        oc * jax.nn.silu(m['cg']),
        om.reshape(B_, T, GROUP_W) * jax.nn.silu(m['mg']),
    ], axis=-1)
    return x + rmsnorm(cat @ p['w_out'], p['g_post'])


def prompt_attention(m):
    aq, ak, av, aqi, aki, awi = m['aq'], m['ak'], m['av'], m['aqi'], m['aki'], m['awi']
    q_lat, q_rope, c, kr = m['q_lat'], m['q_rope'], m['c'], m['kr']
    B_, S = aq.shape[0], aq.shape[1]
    topk = min(TOPK_MAX, S // 4)
    key_pos = jnp.arange(S)
    gather = jax.vmap(lambda rows, idx: rows[idx])

    def block(i):
        start = i * Q_BLOCK
        sl = lambda t: lax.dynamic_slice_in_dim(t, start, Q_BLOCK, axis=1)
        mask = key_pos[None, :] <= (start + jnp.arange(Q_BLOCK))[:, None]
        sel, valid = indexer_topk(sl(aqi), aki, sl(awi), mask, topk)
        oa = sparse_attend(sl(aq), gather(ak, sel), gather(av, sel), valid)
        oc = latent_attend(sl(q_lat), sl(q_rope), c, kr, mask)
        return oa, oc

    oa, oc = lax.map(block, jnp.arange(S // Q_BLOCK))
    oa = jnp.moveaxis(oa, 0, 1).reshape(B_, S, A_HEADS, A_HD)
    oc = jnp.moveaxis(oc, 0, 1).reshape(B_, S, C_HEADS, C_KV_RANK)
    return oa, oc


def sample_attention(m, page_table, pool_ak, pool_av, pool_ai, pool_c, pool_kr):
    T = m['aq'].shape[1]
    L = PAST_LEN + T
    topk = min(TOPK_MAX, L // 4)
    flat_k = pool_ak.reshape(-1, A_HD)
    flat_v = pool_av.reshape(-1, A_HD)
    causal = jnp.arange(L)[None, :] <= (PAST_LEN + jnp.arange(T))[:, None]

    def per_seq(args):
        pt, q, k, v, qi, ki, wi, ql, qr, cc, kk = args
        ki_all = jnp.concatenate([pool_ai[pt].reshape(PAST_LEN, IDX_DIM), ki], axis=0)
        sel, valid = indexer_topk(qi, ki_all, wi, causal, topk)
        is_past = (sel < PAST_LEN)[..., None]
        pidx = jnp.minimum(sel, PAST_LEN - 1)
        phys = pt[pidx // PAGE_SIZE] * PAGE_SIZE + pidx % PAGE_SIZE
        nidx = jnp.clip(sel - PAST_LEN, 0, T - 1)
        k_sel = jnp.where(is_past, flat_k[phys], k[nidx])
        v_sel = jnp.where(is_past, flat_v[phys], v[nidx])
        oa = sparse_attend(q, k_sel, v_sel, valid)
        c_all = jnp.concatenate([pool_c[pt].reshape(PAST_LEN, C_KV_RANK), cc], axis=0)
        kr_all = jnp.concatenate([pool_kr[pt].reshape(PAST_LEN, C_ROPE), kk], axis=0)
        oc = latent_attend(ql, qr, c_all, kr_all, causal)
        return oa, oc

    return lax.map(per_seq, (page_table, m['aq'], m['ak'], m['av'], m['aqi'], m['aki'], m['awi'],
                             m['q_lat'], m['q_rope'], m['c'], m['kr']))


def layer_prompt(x, mem, p):
    B_, S = x.shape[0], x.shape[1]
    m = mixer_inputs(x, p, jnp.arange(S, dtype=jnp.float32))
    oa, oc_lat = prompt_attention(m)
    ob, b_conv, b_h = rglru(m['bx'], jnp.zeros((B_, CONV_W - 1, B_W), x.dtype),
                            jnp.zeros((B_, B_W), x.dtype), p)
    mk = (mem @ p['w_mk']).reshape(B_, N_MEM, M_HEADS, M_HD)
    mv = (mem @ p['w_mv']).reshape(B_, N_MEM, M_HEADS, M_HD)
    om = mem_attend(m['mq'], mk, mv)
    y = mixer_output(x, m, oa, ob, oc_lat, om, p)
    return y, (m['ak'], m['av'], m['aki'], m['c'], m['kr'], b_h, b_conv, mk, mv)


def layer_sample(x, page_table, pool_ak, pool_av, pool_ai, pool_c, pool_kr, b_h, b_conv, mk, mv, p):
    T = x.shape[1]
    m = mixer_inputs(x, p, PAST_LEN + jnp.arange(T, dtype=jnp.float32))
    oa, oc_lat = sample_attention(m, page_table, pool_ak, pool_av, pool_ai, pool_c, pool_kr)
    ob, new_conv, new_h = rglru(m['bx'], b_conv, b_h, p)
    om = mem_attend(m['mq'], mk, mv)
    y = mixer_output(x, m, oa, ob, oc_lat, om, p)
    return y, (m['ak'], m['av'], m['aki'], m['c'], m['kr'], new_h, new_conv)


def setup_inputs(seed: int = 0) -> dict:
    key = jax.random.key(seed)
    ks = jax.random.split(key, 40)
    n_pages = PAST_LEN // PAGE_SIZE
    n_used = DEC_BATCH * n_pages
    n_pool = (n_used * 5) // 4
    f32 = jnp.float32

    def nrm(k, shape, scale=1.0):
        return jax.random.normal(k, shape, f32) * scale

    page_table = jax.random.permutation(ks[0], n_pool)[:n_used].reshape(DEC_BATCH, n_pages).astype(jnp.int32)
    a0 = jax.random.uniform(ks[1], (DEPTH, B_W), f32, minval=0.9, maxval=0.999)
    root = a0 ** (1.0 / LRU_C)
    lam = jnp.log(root) - jnp.log1p(-root)
    return {
        'x_prompt': nrm(ks[2], (BATCH, SEQ, D_MODEL)),
        'x_sample': nrm(ks[3], (DEC_BATCH, DEC_SEQ, D_MODEL)),
        'mem_prompt': nrm(ks[4], (BATCH, N_MEM, D_MODEL)),
        'cache_a_k': nrm(ks[5], (DEPTH, n_pool, PAGE_SIZE, A_HD)),
        'cache_a_v': nrm(ks[6], (DEPTH, n_pool, PAGE_SIZE, A_HD)),
        'cache_a_idx': nrm(ks[7], (DEPTH, n_pool, PAGE_SIZE, IDX_DIM)),
        'cache_c_lat': nrm(ks[8], (DEPTH, n_pool, PAGE_SIZE, C_KV_RANK)),
        'cache_c_kr': nrm(ks[9], (DEPTH, n_pool, PAGE_SIZE, C_ROPE)),
        'state_b_h': nrm(ks[10], (DEPTH, DEC_BATCH, B_W), 0.5),
        'state_b_conv': nrm(ks[11], (DEPTH, DEC_BATCH, CONV_W - 1, B_W)),
        'cache_mem_k': nrm(ks[12], (DEPTH, DEC_BATCH, N_MEM, M_HEADS, M_HD)),
        'cache_mem_v': nrm(ks[13], (DEPTH, DEC_BATCH, N_MEM, M_HEADS, M_HD)),
        'page_table': page_table,
        'g_pre': 1.0 + nrm(ks[14], (DEPTH, D_MODEL), 0.01),
        'g_post': 1.0 + nrm(ks[15], (DEPTH, D_MODEL), 0.01),
        'w_in': nrm(ks[16], (DEPTH, D_MODEL, IN_DIM), D_MODEL ** -0.5),
        'w_out': nrm(ks[17], (DEPTH, MIX_W, D_MODEL), MIX_W ** -0.5),
        'conv_w': nrm(ks[18], (DEPTH, CONV_W, B_W), CONV_W ** -0.5),
        'conv_b': nrm(ks[19], (DEPTH, B_W), 0.01),
        'w_ra': nrm(ks[20], (DEPTH, B_BLOCKS, B_BD, B_BD), B_BD ** -0.5),
        'b_ra': nrm(ks[21], (DEPTH, B_W), 0.01),
        'w_ri': nrm(ks[22], (DEPTH, B_BLOCKS, B_BD, B_BD), B_BD ** -0.5),
        'b_ri': nrm(ks[23], (DEPTH, B_W), 0.01),
        'lam': lam,
        'g_q': 1.0 + nrm(ks[24], (DEPTH, C_Q_RANK), 0.01),
        'w_uq': nrm(ks[25], (DEPTH, C_Q_RANK, C_HEADS * (C_NOPE + C_ROPE)), C_Q_RANK ** -0.5),
        'g_kv': 1.0 + nrm(ks[26], (DEPTH, C_KV_RANK), 0.01),
        'w_uk': nrm(ks[27], (DEPTH, C_KV_RANK, C_HEADS, C_NOPE), C_KV_RANK ** -0.5),
        'w_uv': nrm(ks[28], (DEPTH, C_KV_RANK, C_HEADS, C_VD), C_KV_RANK ** -0.5),
        'w_mk': nrm(ks[29], (DEPTH, D_MODEL, M_HEADS * M_HD), D_MODEL ** -0.5),
        'w_mv': nrm(ks[30], (DEPTH, D_MODEL, M_HEADS * M_HD), D_MODEL ** -0.5),
    }


def reference(x_prompt, x_sample, mem_prompt, cache_a_k, cache_a_v, cache_a_idx, cache_c_lat,
              cache_c_kr, state_b_h, state_b_conv, cache_mem_k, cache_mem_v, page_table,
              g_pre, g_post, w_in, w_out, conv_w, conv_b, w_ra, b_ra, w_ri, b_ri, lam,
              g_q, w_uq, g_kv, w_uk, w_uv, w_mk, w_mv):
    xp, xs = x_prompt, x_sample
    prompt_states = [[] for _ in range(9)]
    sample_states = [[] for _ in range(7)]
    for l in range(DEPTH):
        p = {'g_pre': g_pre[l], 'g_post': g_post[l], 'w_in': w_in[l], 'w_out': w_out[l],
             'conv_w': conv_w[l], 'conv_b': conv_b[l], 'w_ra': w_ra[l], 'b_ra': b_ra[l],
             'w_ri': w_ri[l], 'b_ri': b_ri[l], 'lam': lam[l], 'g_q': g_q[l], 'w_uq': w_uq[l],
             'g_kv': g_kv[l], 'w_uk': w_uk[l], 'w_uv': w_uv[l], 'w_mk': w_mk[l], 'w_mv': w_mv[l]}
        xp, st_p = layer_prompt(xp, mem_prompt, p)
        xs, st_s = layer_sample(xs, page_table, cache_a_k[l], cache_a_v[l], cache_a_idx[l],
                                cache_c_lat[l], cache_c_kr[l], state_b_h[l], state_b_conv[l],
                                cache_mem_k[l], cache_mem_v[l], p)
        for lst, v in zip(prompt_states, st_p):
            lst.append(v)
        for lst, v in zip(sample_states, st_s):
            lst.append(v)
    (p_a_k, p_a_v, p_a_idx, p_c_lat, p_c_kr, p_b_h, p_b_conv, p_mem_k, p_mem_v) = [jnp.stack(v) for v in prompt_states]
    (s_a_k, s_a_v, s_a_idx, s_c_lat, s_c_kr, s_b_h, s_b_conv) = [jnp.stack(v) for v in sample_states]
    return (xp, xs, p_a_k, p_a_v, p_a_idx, p_c_lat, p_c_kr, p_b_h, p_b_conv, p_mem_k, p_mem_v,
            s_a_k, s_a_v, s_a_idx, s_c_lat, s_c_kr, s_b_h, s_b_conv)
```

```python
import functools

import numpy as np
import jax
import jax.numpy as jnp
from jax import lax
from jax.experimental import pallas as pl
from jax.experimental.pallas import tpu as pltpu

F32 = jnp.float32
BF16 = jnp.bfloat16
I32 = jnp.int32

EPS = 1e-6
LRU_C = 8.0
ROPE_BASE = 10000.0
TOPK_MAX = 256
MASK = -1e30
INT_MIN = int(np.iinfo(np.int32).min)

LANES = 128
HEADS = 4
HD = 64
GW = HEADS * HD
CK = 512
VMEM_LIMIT = 48 * 1024 * 1024
PAGES_PER_STEP = 8

_UM = 8 * GW
_O_AK, _O_AV, _O_AKI, _O_AWI = _UM, _UM + 128, _UM + 256, _UM + 384
_O_CDQ = _UM + 512
_O_CDKV = _O_CDQ + 256
_O_KR = _O_CDKV + 128
_NP = _O_KR + 128

_NT = (((1,), (1,)), ((), ()))


def _cparams(n_axes):
    return pltpu.CompilerParams(dimension_semantics=("arbitrary",) * n_axes,
                                vmem_limit_bytes=VMEM_LIMIT)


def _split_heads_rows(a, dtype=BF16):
    return jnp.concatenate([a[:, h * HD:(h + 1) * HD] for h in range(HEADS)], axis=0).astype(dtype)


def _tile_rows(a, n):
    return jnp.concatenate([a] * n, axis=0)


def _in_proj_kernel(x_ref, gpre_ref, w_ref, gq_ref, wq2_ref, wuk_ref, gkv_ref, freq_ref, hm_ref,
                    um_ref, ak_ref, av_ref, aki_ref, awi_ref, c_ref, kr_ref, qcat_ref, kcat_ref,
                    *, tm, pos_base, pos_mod, q_rank, kv_rank, idx_scale):
    i = pl.program_id(0)
    x = x_ref[...]
    h = x * lax.rsqrt(jnp.mean(x * x, axis=-1, keepdims=True) + EPS) * gpre_ref[...]
    u = jnp.dot(h.astype(BF16), w_ref[...], preferred_element_type=F32)
    um_ref[...] = u[:, :_UM]
    ak_ref[...] = u[:, _O_AK:_O_AK + HD]
    av_ref[...] = u[:, _O_AV:_O_AV + HD]
    aki_ref[...] = u[:, _O_AKI:_O_AKI + HD]
    awi_ref[...] = u[:, _O_AWI:_O_AWI + 128] * idx_scale

    cdq = u[:, _O_CDQ:_O_CDQ + 256]
    cq = cdq * lax.rsqrt(jnp.sum(cdq * cdq, axis=-1, keepdims=True) * (1.0 / q_rank) + EPS) * gq_ref[...]
    qf = jnp.dot(cq.astype(BF16), wq2_ref[...], preferred_element_type=F32)
    qlat = jnp.dot(qf[:, :GW].astype(BF16), wuk_ref[...], preferred_element_type=F32)

    cdkv = u[:, _O_CDKV:_O_CDKV + kv_rank]
    c = cdkv * lax.rsqrt(jnp.mean(cdkv * cdkv, axis=-1, keepdims=True) + EPS) * gkv_ref[...]
    c_ref[...] = c

    row = i * tm + lax.broadcasted_iota(I32, (tm, 1), 0)
    pos = (pos_base + lax.rem(row, pos_mod)).astype(F32)
    ang = pos * freq_ref[...]
    cos = jnp.cos(ang)
    sin = jnp.sin(ang)

    def rot(a):
        x1, x2 = a[:, :64], a[:, 64:]
        return x1 * cos - x2 * sin, x1 * sin + x2 * cos

    q1, q2 = rot(qf[:, GW:GW + 128])
    k1, k2 = rot(u[:, _O_KR:_O_KR + 128])
    kr_ref[...] = jnp.concatenate([k1[:, :16], k2[:, :16]], axis=1)
    kcat_ref[:, 0:kv_rank] = c
    kcat_ref[:, kv_rank:kv_rank + 64] = k1
    kcat_ref[:, kv_rank + 64:kv_rank + 128] = k2
    qr = jnp.concatenate([q1, q2], axis=1)
    for hh in range(HEADS):
        qcat_ref[hh, :, 0:kv_rank] = qlat[:, hh * kv_rank:(hh + 1) * kv_rank]
        qcat_ref[hh, :, kv_rank:kv_rank + 128] = qr * hm_ref[hh:hh + 1, :]


def _in_proj(x2d, wts, *, tm, pos_base, pos_mod):
    r, d = x2d.shape
    kv_rank = wts["kv_rank"]
    kern = functools.partial(_in_proj_kernel, tm=tm, pos_base=pos_base, pos_mod=pos_mod,
                             q_rank=wts["q_rank"], kv_rank=kv_rank, idx_scale=HEADS ** -0.5)
    const = lambda shape: pl.BlockSpec(shape, lambda i: (0,) * len(shape))
    rows = lambda w: pl.BlockSpec((tm, w), lambda i: (i, 0))
    out_shape = (
        jax.ShapeDtypeStruct((r, _UM), F32),
        jax.ShapeDtypeStruct((r, HD), F32), jax.ShapeDtypeStruct((r, HD), F32),
        jax.ShapeDtypeStruct((r, HD), F32), jax.ShapeDtypeStruct((r, 128), F32),
        jax.ShapeDtypeStruct((r, kv_rank), F32), jax.ShapeDtypeStruct((r, 32), F32),
        jax.ShapeDtypeStruct((HEADS, r, kv_rank + 128), F32),
        jax.ShapeDtypeStruct((r, kv_rank + 128), F32),
    )
    out_specs = (rows(_UM), rows(HD), rows(HD), rows(HD), rows(128), rows(kv_rank), rows(32),
                 pl.BlockSpec((HEADS, tm, kv_rank + 128), lambda i: (0, i, 0)), rows(kv_rank + 128))
    return pl.pallas_call(
        kern, grid=(r // tm,),
        in_specs=[rows(d), const((1, d)), const((d, _NP)), const((1, 256)), const((256, GW + 128)),
                  const((GW, HEADS * kv_rank)), const((1, kv_rank)), const((1, 64)), const((8, 128))],
        out_specs=out_specs, out_shape=out_shape, compiler_params=_cparams(1), name="in_proj",
    )(x2d, wts["g_pre"], wts["w_in"], wts["g_q"], wts["w_q2"], wts["w_uk_bd"], wts["g_kv"],
      wts["freqs"], wts["head_mask"])


def _score_to_key(sc):
    sc = jnp.where(sc == 0.0, 0.0, sc)
    bits = lax.bitcast_convert_type(sc, I32)
    return bits ^ (lax.shift_right_arithmetic(bits, 31) & 0x7FFFFFFF)


def _count(keys_ref, nch, pred):
    rows, ck = keys_ref.shape[1], keys_ref.shape[2]

    def body(c, acc):
        x = keys_ref[c]
        for j in range(ck // LANES):
            acc = acc + jnp.where(pred(x[:, j * LANES:(j + 1) * LANES]), 1.0, 0.0)
        return acc

    acc = lax.fori_loop(0, nch, body, jnp.zeros((rows, LANES), F32))
    return jnp.sum(acc, axis=1, keepdims=True)


def _kth_largest_key(keys_ref, nch, k):
    rows = keys_ref.shape[1]
    kf = float(k)
    bcast = lambda col: jnp.broadcast_to(col, (rows, LANES))
    zero = jnp.zeros((rows, LANES), I32)
    t0 = jnp.where(bcast(_count(keys_ref, nch, lambda x: x >= zero)) >= kf, 0, INT_MIN).astype(I32)

    def body(it, t):
        cand = t | lax.shift_left(jnp.int32(1), (30 - it).astype(I32))
        tot = bcast(_count(keys_ref, nch, lambda x: x >= cand))
        return jnp.where(tot >= kf, cand, t)

    return lax.fori_loop(0, 31, body, t0)


def _select_chunk(x, t1, need, carry, tri):
    valid = x != INT_MIN
    eq = (x == t1) & valid
    eqf = jnp.where(eq, 1.0, 0.0)
    rank = jnp.dot(eqf.astype(BF16), tri, preferred_element_type=F32) + carry
    sel = (x > t1) | (eq & (rank < need))
    return sel, carry + jnp.sum(eqf, axis=1, keepdims=True)


def _softmax_update(m_ref, l_ref, acc_ref, s, vmat):
    m_prev = m_ref[...]
    m_new = jnp.maximum(m_prev, jnp.max(s, axis=1, keepdims=True))
    alpha = jnp.exp(m_prev - m_new)
    p = jnp.exp(s - m_new)
    l_ref[...] = alpha * l_ref[...] + jnp.sum(p, axis=1, keepdims=True)
    acc_ref[...] = alpha * acc_ref[...] + jnp.dot(p.astype(BF16), vmat, preferred_element_type=F32)
    m_ref[...] = m_new


def _softmax_init(m_ref, l_ref, acc_ref):
    m_ref[...] = jnp.full(m_ref.shape, MASK, F32)
    l_ref[...] = jnp.zeros(l_ref.shape, F32)
    acc_ref[...] = jnp.zeros(acc_ref.shape, F32)


def _dsa_prompt_kernel(qi_ref, q_ref, wi_ref, ki_ref, k_ref, v_ref, tri_ref, o_ref,
                       keys_ref, m_ref, l_ref, acc_ref, *, tq, topk):
    q0 = pl.program_id(1) * tq
    nch = (q0 + tq + CK - 1) // CK
    scale = HD ** -0.5
    qi4 = _split_heads_rows(qi_ref[0])
    wi = wi_ref[0]
    wcols = [wi[:, h:h + 1] for h in range(HEADS)]
    rowpos = q0 + lax.broadcasted_iota(I32, (tq, CK), 0)
    colpos0 = lax.broadcasted_iota(I32, (tq, CK), 1)

    def fill(c, carry):
        off = pl.multiple_of(c * CK, CK)
        kic = ki_ref[0, pl.ds(off, CK), :].astype(BF16)
        s = lax.dot_general(qi4, kic, _NT, preferred_element_type=F32)
        sc = jnp.zeros((tq, CK), F32)
        for h in range(HEADS):
            sc = sc + jnp.maximum(s[h * tq:(h + 1) * tq] * scale, 0.0) * wcols[h]
        key = jnp.where(colpos0 + off <= rowpos, _score_to_key(sc), INT_MIN)
        keys_ref[c] = key
        return carry

    lax.fori_loop(0, nch, fill, 0)

    t = _kth_largest_key(keys_ref, nch, topk)
    need = float(topk) - _count(keys_ref, nch, lambda x: x > t)
    t1 = t[:, :1]

    q4 = _split_heads_rows(q_ref[0])
    _softmax_init(m_ref, l_ref, acc_ref)

    def attend(c, carry):
        off = pl.multiple_of(c * CK, CK)
        sel, carry = _select_chunk(keys_ref[c], t1, need, carry, tri_ref[...])
        bias = jnp.where(sel, 0.0, MASK)
        kc = k_ref[0, pl.ds(off, CK), :].astype(BF16)
        vc = v_ref[0, pl.ds(off, CK), :].astype(BF16)
        s = lax.dot_general(q4, kc, _NT, preferred_element_type=F32) * scale + _tile_rows(bias, HEADS)
        _softmax_update(m_ref, l_ref, acc_ref, s, vc)
        return carry

    lax.fori_loop(0, nch, attend, jnp.zeros((tq, 1), F32))
    o = acc_ref[...] / l_ref[...]
    for h in range(HEADS):
        o_ref[0, :, h * HD:(h + 1) * HD] = o[h * tq:(h + 1) * tq]


def _dsa_prompt(um, awi, aki, ak, av, tri, *, tq):
    b, s, _ = um.shape
    topk = min(TOPK_MAX, s // 4)
    kern = functools.partial(_dsa_prompt_kernel, tq=tq, topk=topk)
    full = lambda w: pl.BlockSpec((1, s, w), lambda bi, i: (bi, 0, 0))
    return pl.pallas_call(
        kern, grid=(b, s // tq),
        in_specs=[pl.BlockSpec((1, tq, GW), lambda bi, i: (bi, i, 1)),
                  pl.BlockSpec((1, tq, GW), lambda bi, i: (bi, i, 0)),
                  pl.BlockSpec((1, tq, 128), lambda bi, i: (bi, i, 0)),
                  full(HD), full(HD), full(HD),
                  pl.BlockSpec((CK, CK), lambda bi, i: (0, 0))],
        out_specs=pl.BlockSpec((1, tq, GW), lambda bi, i: (bi, i, 0)),
        out_shape=jax.ShapeDtypeStruct((b, s, GW), F32),
        scratch_shapes=[pltpu.VMEM((s // CK, tq, CK), I32), pltpu.VMEM((HEADS * tq, 1), F32),
                        pltpu.VMEM((HEADS * tq, 1), F32), pltpu.VMEM((HEADS * tq, HD), F32)],
        compiler_params=_cparams(2), name="dsa_prompt",
    )(um, um, awi, aki, ak, av, tri)


def _mla_prompt_kernel(q_ref, kc_ref, wuv_ref, o_ref, m_ref, l_ref, acc_ref, *, tq, kv_rank, scale):
    q0 = pl.program_id(1) * tq
    nch = (q0 + tq + CK - 1) // CK
    q = q_ref[:, 0].reshape(HEADS * tq, q_ref.shape[-1]).astype(BF16)
    rowpos = q0 + lax.broadcasted_iota(I32, (tq, CK), 0)
    colpos0 = lax.broadcasted_iota(I32, (tq, CK), 1)
    _softmax_init(m_ref, l_ref, acc_ref)

    def body(c, carry):
        off = pl.multiple_of(c * CK, CK)
        kc = kc_ref[0, pl.ds(off, CK), :].astype(BF16)
        bias = jnp.where(colpos0 + off <= rowpos, 0.0, MASK)
        s = lax.dot_general(q, kc, _NT, preferred_element_type=F32) * scale + _tile_rows(bias, HEADS)
        _softmax_update(m_ref, l_ref, acc_ref, s, kc[:, :kv_rank])
        return carry

    lax.fori_loop(0, nch, body, 0)
    o = acc_ref[...] / l_ref[...]
    for h in range(HEADS):
        oh = jnp.dot(o[h * tq:(h + 1) * tq].astype(BF16), wuv_ref[h], preferred_element_type=F32)
        o_ref[0, :, h * HD:(h + 1) * HD] = oh


def _mla_prompt(qcat, kcat, wuv, *, tq, scale):
    _, b, s, w = qcat.shape
    kv_rank = wuv.shape[1]
    kern = functools.partial(_mla_prompt_kernel, tq=tq, kv_rank=kv_rank, scale=scale)
    return pl.pallas_call(
        kern, grid=(b, s // tq),
        in_specs=[pl.BlockSpec((HEADS, 1, tq, w), lambda bi, i: (0, bi, i, 0)),
                  pl.BlockSpec((1, s, w), lambda bi, i: (bi, 0, 0)),
                  pl.BlockSpec((HEADS, kv_rank, HD), lambda bi, i: (0, 0, 0))],
        out_specs=pl.BlockSpec((1, tq, GW), lambda bi, i: (bi, i, 0)),
        out_shape=jax.ShapeDtypeStruct((b, s, GW), F32),
        scratch_shapes=[pltpu.VMEM((HEADS * tq, 1), F32), pltpu.VMEM((HEADS * tq, 1), F32),
                        pltpu.VMEM((HEADS * tq, kv_rank), F32)],
        compiler_params=_cparams(2), name="mla_prompt",
    )(qcat, kcat, wuv)


def _rglru_kernel(x_ref, h0_ref, c0_ref, cw_ref, cb_ref, wra_ref, bra_ref, wri_ref, bri_ref, lam_ref,
                  o_ref, hl_ref, xbuf, a_s, u_s, hst, *, tc, pad):
    @pl.when(pl.program_id(1) == 0)
    def _():
        hst[...] = h0_ref[0]
        xbuf[5:8, :] = c0_ref[0]

    x = x_ref[0]
    xbuf[8:8 + tc, :] = x
    cw = cw_ref[...]
    y = (cb_ref[...] + cw[0:1] * xbuf[5:5 + tc, :] + cw[1:2] * xbuf[6:6 + tc, :]
         + cw[2:3] * xbuf[7:7 + tc, :] + cw[3:4] * x)
    xbuf[5:8, :] = x[tc - 3:tc, :]

    yb = y.astype(BF16)
    r = jax.nn.sigmoid(jnp.dot(yb, wra_ref[...], preferred_element_type=F32) + bra_ref[...])
    ig = jax.nn.sigmoid(jnp.dot(yb, wri_ref[...], preferred_element_type=F32) + bri_ref[...])
    z = -lam_ref[...]
    softplus = jnp.maximum(z, 0.0) + jnp.log1p(jnp.exp(-jnp.abs(z)))
    log_a = -LRU_C * r * softplus
    a = jnp.exp(log_a)
    u = jnp.sqrt(1.0 - a * a) * ig * y

    w = x.shape[1]
    a_s[0:pad, :] = jnp.ones((pad, w), F32)
    u_s[0:pad, :] = jnp.zeros((pad, w), F32)
    a_s[pad:pad + tc, :] = a
    u_s[pad:pad + tc, :] = u
    sh = 1
    while sh < tc:
        a_c = a_s[pad:pad + tc, :]
        u_c = u_s[pad:pad + tc, :]
        a_p = a_s[pad - sh:pad - sh + tc, :]
        u_p = u_s[pad - sh:pad - sh + tc, :]
        u_s[pad:pad + tc, :] = a_c * u_p + u_c
        a_s[pad:pad + tc, :] = a_c * a_p
        sh *= 2
    hs = u_s[pad:pad + tc, :] + a_s[pad:pad + tc, :] * hst[...]
    o_ref[0] = hs
    hst[...] = hs[tc - 1:tc, :]
    hl_ref[0] = hs[tc - 1:tc, :]


def _rglru(bx_src, col, h0, c0, wts, *, tc):
    b, t, _ = bx_src.shape
    w = GW
    pad = max(8, tc // 2)
    kern = functools.partial(_rglru_kernel, tc=tc, pad=pad)
    const = lambda shape: pl.BlockSpec(shape, lambda bi, j: (0,) * len(shape))
    return pl.pallas_call(
        kern, grid=(b, t // tc),
        in_specs=[pl.BlockSpec((1, tc, w), lambda bi, j: (bi, j, col)),
                  pl.BlockSpec((1, 1, w), lambda bi, j: (bi, 0, 0)),
                  pl.BlockSpec((1, 3, w), lambda bi, j: (bi, 0, 0)),
                  const((4, w)), const((1, w)), const((w, w)), const((1, w)), const((w, w)),
                  const((1, w)), const((1, w))],
        out_specs=(pl.BlockSpec((1, tc, w), lambda bi, j: (bi, j, 0)),
                   pl.BlockSpec((1, 1, w), lambda bi, j: (bi, 0, 0))),
        out_shape=(jax.ShapeDtypeStruct((b, t, w), F32), jax.ShapeDtypeStruct((b, 1, w), F32)),
        scratch_shapes=[pltpu.VMEM((8 + tc, w), F32), pltpu.VMEM((pad + tc, w), F32),
                        pltpu.VMEM((pad + tc, w), F32), pltpu.VMEM((1, w), F32)],
        compiler_params=_cparams(2), name="rglru",
    )(bx_src, h0.reshape(b, 1, w), c0, wts["conv_w"], wts["conv_b"], wts["w_ra_bd"], wts["b_ra"],
      wts["w_ri_bd"], wts["b_ri"], wts["lam"])


def _matmul_kernel(x_ref, w_ref, o_ref):
    o_ref[...] = jnp.dot(x_ref[...].astype(BF16), w_ref[...], preferred_element_type=F32)


def _matmul(x, w, *, tm):
    r, d = x.shape
    n = w.shape[1]
    return pl.pallas_call(
        _matmul_kernel, grid=(r // tm,),
        in_specs=[pl.BlockSpec((tm, d), lambda i: (i, 0)), pl.BlockSpec((d, n), lambda i: (0, 0))],
        out_specs=pl.BlockSpec((tm, n), lambda i: (i, 0)),
        out_shape=jax.ShapeDtypeStruct((r, n), F32), compiler_params=_cparams(1), name="mem_kv",
    )(x, w)


def _mem_attn_kernel(q_ref, k_ref, v_ref, o_ref):
    q, k, v = q_ref[0], k_ref[0], v_ref[0]
    scale = HD ** -0.5
    for h in range(HEADS):
        sl = slice(h * HD, (h + 1) * HD)
        s = lax.dot_general(q[:, sl].astype(BF16), k[:, sl].astype(BF16), _NT,
                            preferred_element_type=F32) * scale
        p = jnp.exp(s - jnp.max(s, axis=1, keepdims=True))
        o = jnp.dot(p.astype(BF16), v[:, sl].astype(BF16), preferred_element_type=F32)
        o_ref[0, :, sl] = o / jnp.sum(p, axis=1, keepdims=True)


def _mem_attn(q_src, qcol, k_src, kcol, v_src, vcol, *, tq):
    b, t, _ = q_src.shape
    n = k_src.shape[1]
    return pl.pallas_call(
        _mem_attn_kernel, grid=(b, t // tq),
        in_specs=[pl.BlockSpec((1, tq, GW), lambda bi, i: (bi, i, qcol)),
                  pl.BlockSpec((1, n, GW), lambda bi, i: (bi, 0, kcol)),
                  pl.BlockSpec((1, n, GW), lambda bi, i: (bi, 0, vcol))],
        out_specs=pl.BlockSpec((1, tq, GW), lambda bi, i: (bi, i, 0)),
        out_shape=jax.ShapeDtypeStruct((b, t, GW), F32), compiler_params=_cparams(2), name="mem_attn",
    )(q_src, k_src, v_src)


def _out_proj_kernel(x_ref, ga_ref, gb_ref, gc_ref, gm_ref, oa_ref, ob_ref, oc_ref, om_ref,
                     w_ref, gpost_ref, y_ref):
    acc = None
    for idx, (o_r, g_r) in enumerate(((oa_ref, ga_ref), (ob_ref, gb_ref), (oc_ref, gc_ref), (om_ref, gm_ref))):
        g = g_r[...]
        part = (o_r[...] * (g * jax.nn.sigmoid(g))).astype(BF16)
        d = jnp.dot(part, w_ref[idx * GW:(idx + 1) * GW, :], preferred_element_type=F32)
        acc = d if acc is None else acc + d
    yn = acc * lax.rsqrt(jnp.mean(acc * acc, axis=-1, keepdims=True) + EPS) * gpost_ref[...]
    y_ref[...] = x_ref[...] + yn


def _out_proj(x2d, um, oa, ob, oc, om, w_out, g_post, *, tm):
    r, d = x2d.shape
    gate = lambda col: pl.BlockSpec((tm, GW), lambda i: (i, col))
    grp = pl.BlockSpec((tm, GW), lambda i: (i, 0))
    return pl.pallas_call(
        _out_proj_kernel, grid=(r // tm,),
        in_specs=[pl.BlockSpec((tm, d), lambda i: (i, 0)), gate(2), gate(4), gate(5), gate(7),
                  grp, grp, grp, grp,
                  pl.BlockSpec((4 * GW, d), lambda i: (0, 0)), pl.BlockSpec((1, d), lambda i: (0, 0))],
        out_specs=pl.BlockSpec((tm, d), lambda i: (i, 0)),
        out_shape=jax.ShapeDtypeStruct((r, d), F32), compiler_params=_cparams(1), name="out_proj",
    )(x2d, um, um, um, um, oa, ob, oc, om, w_out, g_post)


def _sample_idx_kernel(pt_ref, qi_ref, wi_ref, kin_ref, *rest, n_pages_step):
    pages = rest[:n_pages_step]
    scp_ref, scn_ref = rest[n_pages_step:]
    del pt_ref
    scale = HD ** -0.5
    t = qi_ref.shape[1]
    qi4 = _split_heads_rows(qi_ref[0])
    wi = wi_ref[0]
    w_rows = jnp.concatenate([wi[:, h:h + 1] for h in range(HEADS)], axis=0)

    def score(kmat):
        s = lax.dot_general(qi4, kmat, _NT, preferred_element_type=F32)
        r = jnp.maximum(s * scale, 0.0) * w_rows
        out = r[0:t]
        for h in range(1, HEADS):
            out = out + r[h * t:(h + 1) * t]
        return out

    kp = jnp.concatenate([p[...] for p in pages], axis=0).astype(BF16)
    scp_ref[0] = score(kp)

    @pl.when(pl.program_id(1) == pl.num_programs(1) - 1)
    def _():
        kn = jnp.concatenate([kin_ref[0], jnp.zeros((CK - t, HD), F32)], axis=0).astype(BF16)
        sn = score(kn)
        col = lax.broadcasted_iota(I32, (t, CK), 1)
        row = lax.broadcasted_iota(I32, (t, CK), 0)
        scn_ref[0] = jnp.where(col <= row, sn, -jnp.inf)


def _page_specs(layer, page, width, n):
    def spec(j):
        return pl.BlockSpec((None, None, page, width),
                            lambda s, g, pt: (layer, pt[s, g * n + j], 0, 0))
    return [spec(j) for j in range(n)]


def _sample_idx(layer, page_table, um_s, awi_s, aki_s, cache_idx):
    ns, t, _ = um_s.shape
    n_pages = page_table.shape[1]
    page = cache_idx.shape[2]
    p = PAGES_PER_STEP
    kern = functools.partial(_sample_idx_kernel, n_pages_step=p)
    grid_spec = pltpu.PrefetchScalarGridSpec(
        num_scalar_prefetch=1, grid=(ns, n_pages // p),
        in_specs=[pl.BlockSpec((1, t, GW), lambda s, g, pt: (s, 0, 1)),
                  pl.BlockSpec((1, t, 128), lambda s, g, pt: (s, 0, 0)),
                  pl.BlockSpec((1, t, HD), lambda s, g, pt: (s, 0, 0))]
                 + _page_specs(layer, page, HD, p),
        out_specs=(pl.BlockSpec((1, t, p * page), lambda s, g, pt: (s, 0, g)),
                   pl.BlockSpec((1, t, CK), lambda s, g, pt: (s, 0, 0))))
    return pl.pallas_call(
        kern, grid_spec=grid_spec,
        out_shape=(jax.ShapeDtypeStruct((ns, t, n_pages * page), F32),
                   jax.ShapeDtypeStruct((ns, t, CK), F32)),
        compiler_params=_cparams(2), name="sample_idx",
    )(page_table, um_s, awi_s, aki_s, *([cache_idx] * p))


def _topk_bias_kernel(scp_ref, scn_ref, tri_ref, bp_ref, bn_ref, keys_ref, *, topk):
    nchp = scp_ref.shape[1] // CK
    nch = nchp + 1

    def to_key(sc):
        return jnp.where(sc == -jnp.inf, INT_MIN, _score_to_key(sc))

    for c in range(nchp):
        keys_ref[c] = to_key(scp_ref[:, c * CK:(c + 1) * CK])
    keys_ref[nchp] = to_key(scn_ref[...])

    t = _kth_largest_key(keys_ref, nch, topk)
    need = float(topk) - _count(keys_ref, nch, lambda x: x > t)
    t1 = t[:, :1]
    carry = jnp.zeros((keys_ref.shape[1], 1), F32)
    for c in range(nch):
        sel, carry = _select_chunk(keys_ref[c], t1, need, carry, tri_ref[...])
        bias = jnp.where(sel, 0.0, MASK)
        if c < nchp:
            bp_ref[:, c * CK:(c + 1) * CK] = bias
        else:
            bn_ref[...] = bias


def _topk_bias(sc_past, sc_new, tri, *, topk, rb):
    nr, past = sc_past.shape
    kern = functools.partial(_topk_bias_kernel, topk=topk)
    return pl.pallas_call(
        kern, grid=(nr // rb,),
        in_specs=[pl.BlockSpec((rb, past), lambda i: (i, 0)), pl.BlockSpec((rb, CK), lambda i: (i, 0)),
                  pl.BlockSpec((CK, CK), lambda i: (0, 0))],
        out_specs=(pl.BlockSpec((rb, past), lambda i: (i, 0)), pl.BlockSpec((rb, CK), lambda i: (i, 0))),
        out_shape=(jax.ShapeDtypeStruct((nr, past), F32), jax.ShapeDtypeStruct((nr, CK), F32)),
        scratch_shapes=[pltpu.VMEM((past // CK + 1, rb, CK), I32)],
        compiler_params=_cparams(1), name="topk_bias",
    )(sc_past, sc_new, tri)


def _sample_attn_kernel(pt_ref, q_ref, qcat_ref, bp_ref, bn_ref, kn_ref, vn_ref, kcn_ref, et_ref, wuv_ref,
                        *rest, n_pages_step, kv_rank, scale_c):
    del pt_ref
    p = n_pages_step
    kpg, vpg, cpg, rpg = rest[:p], rest[p:2 * p], rest[2 * p:3 * p], rest[3 * p:4 * p]
    oa_ref, oc_ref, m1, l1, acc1, m2, l2, acc2 = rest[4 * p:]
    g = pl.program_id(1)
    t = q_ref.shape[1]
    scale_a = HD ** -0.5

    @pl.when(g == 0)
    def _():
        _softmax_init(m1, l1, acc1)
        _softmax_init(m2, l2, acc2)

    q4 = _split_heads_rows(q_ref[0])
    qc = qcat_ref[:, 0].reshape(HEADS * t, qcat_ref.shape[-1])
    qcb = qc.astype(BF16)
    qlat = qcb[:, :kv_rank]
    qrope = jnp.dot(qcb[:, kv_rank:], et_ref[...], preferred_element_type=F32).astype(BF16)

    cat = lambda refs: jnp.concatenate([r[...] for r in refs], axis=0).astype(BF16)
    kp, vp, cp, rp = cat(kpg), cat(vpg), cat(cpg), cat(rpg)
    s1 = lax.dot_general(q4, kp, _NT, preferred_element_type=F32) * scale_a + _tile_rows(bp_ref[0], HEADS)
    _softmax_update(m1, l1, acc1, s1, vp)
    s2 = (lax.dot_general(qlat, cp, _NT, preferred_element_type=F32)
          + lax.dot_general(qrope, rp, _NT, preferred_element_type=F32)) * scale_c
    _softmax_update(m2, l2, acc2, s2, cp)

    @pl.when(g == pl.num_programs(1) - 1)
    def _():
        zpad = lambda a: jnp.concatenate([a, jnp.zeros((LANES - t, a.shape[1]), F32)], axis=0).astype(BF16)
        kn, vn, kcn = zpad(kn_ref[0]), zpad(vn_ref[0]), zpad(kcn_ref[0])
        s1n = (lax.dot_general(q4, kn, _NT, preferred_element_type=F32) * scale_a
               + _tile_rows(bn_ref[0][:, :LANES], HEADS))
        _softmax_update(m1, l1, acc1, s1n, vn)
        col = lax.broadcasted_iota(I32, (t, LANES), 1)
        row = lax.broadcasted_iota(I32, (t, LANES), 0)
        causal = jnp.where(col <= row, 0.0, MASK)
        s2n = lax.dot_general(qcb, kcn, _NT, preferred_element_type=F32) * scale_c + _tile_rows(causal, HEADS)
        _softmax_update(m2, l2, acc2, s2n, kcn[:, :kv_rank])
        oa = acc1[...] / l1[...]
        oc = acc2[...] / l2[...]
        for h in range(HEADS):
            oa_ref[0, :, h * HD:(h + 1) * HD] = oa[h * t:(h + 1) * t]
            oc_ref[0, :, h * HD:(h + 1) * HD] = jnp.dot(oc[h * t:(h + 1) * t].astype(BF16), wuv_ref[h],
                                                        preferred_element_type=F32)


def _sample_attn(layer, page_table, um_s, qcat_s, bias_past, bias_new, ak_s, av_s, kcat_s, et, wuv,
                 cache_k, cache_v, cache_c, cache_r, *, scale_c):
    ns, t, _ = um_s.shape
    n_pages = page_table.shape[1]
    page = cache_k.shape[2]
    kv_rank = cache_c.shape[3]
    rope = cache_r.shape[3]
    w = qcat_s.shape[-1]
    p = PAGES_PER_STEP
    kern = functools.partial(_sample_attn_kernel, n_pages_step=p, kv_rank=kv_rank, scale_c=scale_c)
    seq = lambda width: pl.BlockSpec((1, t, width), lambda s, g, pt: (s, 0, 0))
    grid_spec = pltpu.PrefetchScalarGridSpec(
        num_scalar_prefetch=1, grid=(ns, n_pages // p),
        in_specs=[seq(GW),
                  pl.BlockSpec((HEADS, 1, t, w), lambda s, g, pt: (0, s, 0, 0)),
                  pl.BlockSpec((1, t, p * page), lambda s, g, pt: (s, 0, g)),
                  seq(CK), seq(HD), seq(HD), seq(w),
                  pl.BlockSpec((128, rope), lambda s, g, pt: (0, 0)),
                  pl.BlockSpec((HEADS, kv_rank, HD), lambda s, g, pt: (0, 0, 0))]
                 + _page_specs(layer, page, HD, p) + _page_specs(layer, page, HD, p)
                 + _page_specs(layer, page, kv_rank, p) + _page_specs(layer, page, rope, p),
        out_specs=(seq(GW), seq(GW)),
        scratch_shapes=[pltpu.VMEM((HEADS * t, 1), F32), pltpu.VMEM((HEADS * t, 1), F32),
                        pltpu.VMEM((HEADS * t, HD), F32), pltpu.VMEM((HEADS * t, 1), F32),
                        pltpu.VMEM((HEADS * t, 1), F32), pltpu.VMEM((HEADS * t, kv_rank), F32)])
    return pl.pallas_call(
        kern, grid_spec=grid_spec,
        out_shape=(jax.ShapeDtypeStruct((ns, t, GW), F32), jax.ShapeDtypeStruct((ns, t, GW), F32)),
        compiler_params=_cparams(2), name="sample_attn",
    )(page_table, um_s, qcat_s, bias_past, bias_new, ak_s, av_s, kcat_s, et, wuv,
      *([cache_k] * p), *([cache_v] * p), *([cache_c] * p), *([cache_r] * p))


def _prep_layer(l, g_pre, g_post, w_in, w_out, conv_w, conv_b, w_ra, b_ra, w_ri, b_ri, lam,
                g_q, w_uq, g_kv, w_uk, w_uv, w_mk, w_mv):
    q_rank = g_q.shape[-1]
    kv_rank = g_kv.shape[-1]
    nope = w_uk.shape[-1]
    rope = w_uq.shape[-1] // HEADS - nope
    half = rope // 2
    assert GW == w_in.shape[1] // 4 and rope == 32 and nope == HD and q_rank <= 256 and kv_rank == 128
    widths = (GW, HD, HD, GW, HD, HEADS, GW, GW, GW, q_rank, kv_rank, rope, GW, GW, GW)
    assert sum(widths) == w_in.shape[-1]
    offs = np.concatenate([[0], np.cumsum(widths)])
    w = w_in[l]
    col = lambda i: w[:, offs[i]:offs[i + 1]]
    padc = lambda a, n: jnp.pad(a, ((0, 0), (0, n - a.shape[1])))
    aq, ak, av, aqi, aki, awi, ag, bx, bg, cdq, cdkv, ckr, cg, mq, mg = [col(i) for i in range(15)]
    kr_dup = jnp.concatenate([ckr[:, :half]] * HEADS + [ckr[:, half:]] * HEADS, axis=1)
    w_all = jnp.concatenate([aq, aqi, ag, bx, bg, cg, mq, mg, padc(ak, 128), padc(av, 128), padc(aki, 128),
                             padc(awi, 128), padc(cdq, 256), cdkv, kr_dup], axis=1)
    assert w_all.shape[1] == _NP

    wq = w_uq[l].reshape(q_rank, HEADS, nope + rope)
    wq2 = jnp.concatenate([wq[:, :, :nope].reshape(q_rank, HEADS * nope),
                           wq[:, :, nope:nope + half].reshape(q_rank, HEADS * half),
                           wq[:, :, nope + half:].reshape(q_rank, HEADS * half)], axis=1)
    wq2 = jnp.pad(wq2, ((0, 256 - q_rank), (0, 0)))
    wuk_bd = jnp.zeros((HEADS * nope, HEADS * kv_rank), F32)
    for h in range(HEADS):
        wuk_bd = wuk_bd.at[h * nope:(h + 1) * nope, h * kv_rank:(h + 1) * kv_rank].set(w_uk[l][:, h, :].T)

    def block_diag(wb):
        nb, bd, _ = wb.shape
        out = jnp.zeros((nb * bd, nb * bd), F32)
        for n in range(nb):
            out = out.at[n * bd:(n + 1) * bd, n * bd:(n + 1) * bd].set(wb[n])
        return out

    freqs = ROPE_BASE ** (-jnp.arange(half, dtype=F32) / half)
    lane = np.arange(128)
    head_mask = np.zeros((8, 128), np.float32)
    for h in range(HEADS):
        head_mask[h] = ((lane % 64) // half == h)
    return {
        "q_rank": q_rank, "kv_rank": kv_rank,
        "g_pre": g_pre[l][None], "g_post": g_post[l][None],
        "w_in": w_all.astype(BF16),
        "g_q": jnp.pad(g_q[l], (0, 256 - q_rank))[None], "w_q2": wq2.astype(BF16),
        "w_uk_bd": wuk_bd.astype(BF16), "g_kv": g_kv[l][None],
        "freqs": jnp.tile(freqs, HEADS)[None], "head_mask": jnp.asarray(head_mask),
        "w_out": w_out[l].astype(BF16),
        "conv_w": conv_w[l], "conv_b": conv_b[l][None],
        "w_ra_bd": block_diag(w_ra[l]).astype(BF16), "b_ra": b_ra[l][None],
        "w_ri_bd": block_diag(w_ri[l]).astype(BF16), "b_ri": b_ri[l][None], "lam": lam[l][None],
        "w_uv": jnp.transpose(w_uv[l], (1, 0, 2)).astype(BF16),
        "w_mkv": jnp.concatenate([w_mk[l], w_mv[l]], axis=1).astype(BF16),
        "scale_c": float((nope + rope) ** -0.5),
    }


def kernel(x_prompt, x_sample, mem_prompt, cache_a_k, cache_a_v, cache_a_idx, cache_c_lat, cache_c_kr,
           state_b_h, state_b_conv, cache_mem_k, cache_mem_v, page_table,
           g_pre, g_post, w_in, w_out, conv_w, conv_b, w_ra, b_ra, w_ri, b_ri, lam,
           g_q, w_uq, g_kv, w_uk, w_uv, w_mk, w_mv):
    b, s, d = x_prompt.shape
    ns, t, _ = x_sample.shape
    depth = w_in.shape[0]
    n_mem = mem_prompt.shape[1]
    past = page_table.shape[1] * cache_a_k.shape[2]
    rope = cache_c_kr.shape[-1]
    assert s % CK == 0 and past % CK == 0 and t == 8 and page_table.shape[1] % PAGES_PER_STEP == 0

    tri = jnp.asarray(np.triu(np.ones((CK, CK), np.float32), k=1), BF16)
    et_np = np.zeros((128, rope), np.float32)
    for lane in range(128):
        et_np[lane, (lane // 64) * (rope // 2) + lane % (rope // 2)] = 1.0
    et = jnp.asarray(et_np, BF16)

    xp = x_prompt.reshape(b * s, d)
    xs = x_sample.reshape(ns * t, d)
    mem2d = mem_prompt.reshape(b * n_mem, d)
    p_states = [[] for _ in range(9)]
    s_states = [[] for _ in range(7)]
    for l in range(depth):
        wts = _prep_layer(l, g_pre, g_post, w_in, w_out, conv_w, conv_b, w_ra, b_ra, w_ri, b_ri, lam,
                          g_q, w_uq, g_kv, w_uk, w_uv, w_mk, w_mv)
        kv_rank = wts["kv_rank"]
        um, ak, av, aki, awi, c, kr, qcat, kcat = _in_proj(xp, wts, tm=512, pos_base=0, pos_mod=s)
        um3 = um.reshape(b, s, _UM)
        r3 = lambda a: a.reshape(b, s, a.shape[-1])
        oa = _dsa_prompt(um3, r3(awi), r3(aki), r3(ak), r3(av), tri, tq=128)
        oc = _mla_prompt(qcat.reshape(HEADS, b, s, kv_rank + 128), r3(kcat), wts["w_uv"], tq=128,
                         scale=wts["scale_c"])
        ob, bh = _rglru(um3, 3, jnp.zeros((b, GW), F32), jnp.zeros((b, 3, GW), F32), wts, tc=512)
        mkv = _matmul(mem2d, wts["w_mkv"], tm=256).reshape(b, n_mem, 2 * GW)
        om = _mem_attn(um3, 6, mkv, 0, mkv, 1, tq=256)
        flat = lambda a: a.reshape(b * s, GW)
        xp_new = _out_proj(xp, um, flat(oa), flat(ob), flat(oc), flat(om), wts["w_out"], wts["g_post"], tm=512)
        st_p = (r3(ak), r3(av), r3(aki), r3(c), r3(kr), bh.reshape(b, GW), um3[:, s - 3:, 3 * GW:4 * GW],
                mkv[:, :, :GW].reshape(b, n_mem, HEADS, HD), mkv[:, :, GW:].reshape(b, n_mem, HEADS, HD))
        um_s, ak_s, av_s, aki_s, awi_s, c_s, kr_s, qcat_s, kcat_s = _in_proj(
            xs, wts, tm=min(512, ns * t), pos_base=past, pos_mod=t)
        um_s3 = um_s.reshape(ns, t, _UM)
        q3 = lambda a: a.reshape(ns, t, a.shape[-1])
        sc_past, sc_new = _sample_idx(l, page_table, um_s3, q3(awi_s), q3(aki_s), cache_a_idx)
        topk = min(TOPK_MAX, (past + t) // 4)
        bias_past, bias_new = _topk_bias(sc_past.reshape(ns * t, past), sc_new.reshape(ns * t, CK), tri,
                                         topk=topk, rb=min(64, ns * t))
        oa_s, oc_s = _sample_attn(l, page_table, um_s3, qcat_s.reshape(HEADS, ns, t, kv_rank + 128),
                                  bias_past.reshape(ns, t, past), bias_new.reshape(ns, t, CK),
                                  q3(ak_s), q3(av_s), q3(kcat_s), et, wts["w_uv"],
                                  cache_a_k, cache_a_v, cache_c_lat, cache_c_kr, scale_c=wts["scale_c"])
        ob_s, bh_s = _rglru(um_s3, 3, state_b_h[l], state_b_conv[l], wts, tc=t)
        om_s = _mem_attn(um_s3, 6, cache_mem_k[l].reshape(ns, n_mem, GW), 0,
                         cache_mem_v[l].reshape(ns, n_mem, GW), 0, tq=t)
        flat_s = lambda a: a.reshape(ns * t, GW)
        xs_new = _out_proj(xs, um_s, flat_s(oa_s), flat_s(ob_s), flat_s(oc_s), flat_s(om_s),
                           wts["w_out"], wts["g_post"], tm=min(512, ns * t))
        st_s = (q3(ak_s), q3(av_s), q3(aki_s), q3(c_s), q3(kr_s), bh_s.reshape(ns, GW),
                um_s3[:, t - 3:, 3 * GW:4 * GW])
        xp, xs = xp_new, xs_new
        for lst, v in zip(p_states, st_p):
            lst.append(v)
        for lst, v in zip(s_states, st_s):
            lst.append(v)
    outs_p = [jnp.stack(v) for v in p_states]
    outs_s = [jnp.stack(v) for v in s_states]
    return (xp.reshape(b, s, d), xs.reshape(ns, t, d), *outs_p, *outs_s)
```

```python
import functools

import numpy as np
import jax
import jax.numpy as jnp
from jax import lax
from jax.experimental import pallas as pl
from jax.experimental.pallas import tpu as pltpu

F32 = jnp.float32
BF16 = jnp.bfloat16
I32 = jnp.int32

EPS = 1e-6
LRU_C = 8.0
ROPE_BASE = 10000.0
ROPE_HALF = 16
_LN_BASE_HI = float(np.float32(np.log(ROPE_BASE)))
_LN_BASE_LO = float(np.float32(np.log(ROPE_BASE) - np.float64(np.float32(np.log(ROPE_BASE)))))
TOPK_MAX = 256
MASK = -1e30
NEG_INF = float("-inf")
INT_MIN = int(np.iinfo(np.int32).min)
KEY_NEG_INF = -2139095041

LANES = 128
HEADS = 4
HD = 64
GW = HEADS * HD
CK = 512
VMEM_LIMIT = 48 * 1024 * 1024
IDX_PAGES_PER_STEP = 64
ATTN_PAGES_PER_STEP = 32

_UM = 8 * GW
_O_AK, _O_AV, _O_AKI, _O_AWI = _UM, _UM + 128, _UM + 256, _UM + 384
_O_CDQ = _UM + 512
_O_CDKV = _O_CDQ + 256
_O_KR = _O_CDKV + 128
_NP = _O_KR + 128

_NT = (((1,), (1,)), ((), ()))


def _cparams(n_axes):
    return pltpu.CompilerParams(dimension_semantics=("arbitrary",) * n_axes,
                                vmem_limit_bytes=VMEM_LIMIT)


def _split_heads_rows(a, dtype=BF16):
    return jnp.concatenate([a[:, h * HD:(h + 1) * HD] for h in range(HEADS)], axis=0).astype(dtype)


def _tile_rows(a, n):
    return jnp.concatenate([a] * n, axis=0)


def _split_bf16(a):
    hi = a.astype(BF16)
    return hi, (a - hi.astype(F32)).astype(BF16)


def _dot3(a, b, dims):
    dot = lambda x, y: lax.dot_general(x, y, dims, preferred_element_type=F32)
    return dot(a[0], b[0]) + (dot(a[1], b[0]) + dot(a[0], b[1]))


def _stack3(a, b, a_axis, b_axis):
    return (jnp.concatenate([a[0], a[1], a[0]], axis=a_axis),
            jnp.concatenate([b[0], b[0], b[1]], axis=b_axis))


def _in_proj_kernel(x_ref, gpre_ref, w_ref, wih_ref, wil_ref, gq_ref, wq2_ref, wuk_ref, gkv_ref, hm_ref,
                    um_ref, ak_ref, av_ref, aki_ref, awi_ref, c_ref, kr_ref, qcat_ref, kcat_ref,
                    *, tm, pos_base, pos_mod, q_rank, kv_rank, idx_scale):
    i = pl.program_id(0)
    x = x_ref[...]
    h = x * lax.rsqrt(jnp.mean(x * x, axis=-1, keepdims=True) + EPS) * gpre_ref[...]
    hs = _split_bf16(h)
    u = jnp.dot(hs[0], w_ref[...], preferred_element_type=F32)
    um_ref[...] = u[:, :_UM]
    ak_ref[...] = u[:, _O_AK:_O_AK + HD]
    av_ref[...] = u[:, _O_AV:_O_AV + HD]
    ui = _dot3(hs, (wih_ref[...], wil_ref[...]), (((1,), (0,)), ((), ())))
    um_ref[:, GW:2 * GW] = ui[:, :GW]
    aki_ref[...] = ui[:, GW:GW + HD]
    awi_ref[...] = ui[:, GW + 128:GW + 256] * idx_scale

    cdq = u[:, _O_CDQ:_O_CDQ + 256]
    cq = cdq * lax.rsqrt(jnp.sum(cdq * cdq, axis=-1, keepdims=True) * (1.0 / q_rank) + EPS) * gq_ref[...]
    qf = jnp.dot(cq.astype(BF16), wq2_ref[...], preferred_element_type=F32)
    qlat = jnp.dot(qf[:, :GW].astype(BF16), wuk_ref[...], preferred_element_type=F32)

    cdkv = u[:, _O_CDKV:_O_CDKV + kv_rank]
    c = cdkv * lax.rsqrt(jnp.mean(cdkv * cdkv, axis=-1, keepdims=True) + EPS) * gkv_ref[...]
    c_ref[...] = c

    row = i * tm + lax.broadcasted_iota(I32, (tm, 1), 0)
    pos = (pos_base + lax.rem(row, pos_mod)).astype(F32)
    j = lax.rem(lax.broadcasted_iota(I32, (1, 64), 1), ROPE_HALF).astype(F32)
    e = -(j / ROPE_HALF)
    freqs = jnp.exp(e * _LN_BASE_HI + e * _LN_BASE_LO)
    ang = pos * freqs
    cos = jnp.cos(ang)
    sin = jnp.sin(ang)

    def rot(a):
        x1, x2 = a[:, :64], a[:, 64:]
        return x1 * cos - x2 * sin, x1 * sin + x2 * cos

    q1, q2 = rot(qf[:, GW:GW + 128])
    k1, k2 = rot(u[:, _O_KR:_O_KR + 128])
    kr_ref[...] = jnp.concatenate([k1[:, :16], k2[:, :16]], axis=1)
    kcat_ref[:, 0:kv_rank] = c
    kcat_ref[:, kv_rank:kv_rank + 64] = k1
    kcat_ref[:, kv_rank + 64:kv_rank + 128] = k2
    qr = jnp.concatenate([q1, q2], axis=1)
    for hh in range(HEADS):
        qcat_ref[hh, :, 0:kv_rank] = qlat[:, hh * kv_rank:(hh + 1) * kv_rank]
        qcat_ref[hh, :, kv_rank:kv_rank + 128] = qr * hm_ref[hh:hh + 1, :]


def _in_proj(x2d, wts, *, tm, pos_base, pos_mod):
    r, d = x2d.shape
    kv_rank = wts["kv_rank"]
    kern = functools.partial(_in_proj_kernel, tm=tm, pos_base=pos_base, pos_mod=pos_mod,
                             q_rank=wts["q_rank"], kv_rank=kv_rank, idx_scale=HEADS ** -0.5)
    const = lambda shape: pl.BlockSpec(shape, lambda i: (0,) * len(shape))
    rows = lambda w: pl.BlockSpec((tm, w), lambda i: (i, 0))
    out_shape = (
        jax.ShapeDtypeStruct((r, _UM), F32),
        jax.ShapeDtypeStruct((r, HD), F32), jax.ShapeDtypeStruct((r, HD), F32),
        jax.ShapeDtypeStruct((r, HD), F32), jax.ShapeDtypeStruct((r, 128), F32),
        jax.ShapeDtypeStruct((r, kv_rank), F32), jax.ShapeDtypeStruct((r, 32), F32),
        jax.ShapeDtypeStruct((HEADS, r, kv_rank + 128), F32),
        jax.ShapeDtypeStruct((r, kv_rank + 128), F32),
    )
    out_specs = (rows(_UM), rows(HD), rows(HD), rows(HD), rows(128), rows(kv_rank), rows(32),
                 pl.BlockSpec((HEADS, tm, kv_rank + 128), lambda i: (0, i, 0)), rows(kv_rank + 128))
    return pl.pallas_call(
        kern, grid=(r // tm,),
        in_specs=[rows(d), const((1, d)), const((d, _NP)), const((d, GW + 256)), const((d, GW + 256)),
                  const((1, 256)), const((256, GW + 128)),
                  const((GW, HEADS * kv_rank)), const((1, kv_rank)), const((8, 128))],
        out_specs=out_specs, out_shape=out_shape, compiler_params=_cparams(1), name="in_proj",
    )(x2d, wts["g_pre"], wts["w_in"], wts["w_idx_hi"], wts["w_idx_lo"], wts["g_q"], wts["w_q2"],
      wts["w_uk_bd"], wts["g_kv"], wts["head_mask"])


def _key_to_float(key):
    bits = key ^ (lax.shift_right_arithmetic(key, 31) & 0x7FFFFFFF)
    return lax.bitcast_convert_type(bits, F32)


def _count(sc_ref, nch, pred):
    rows, ck = sc_ref.shape[1], sc_ref.shape[2]

    def body(c, acc):
        x = sc_ref[c]
        for j in range(ck // LANES):
            acc = acc + jnp.where(pred(x[:, j * LANES:(j + 1) * LANES]), 1.0, 0.0)
        return acc

    acc = lax.fori_loop(0, nch, body, jnp.zeros((rows, LANES), F32))
    return jnp.sum(acc, axis=1, keepdims=True)


def _kth_largest(sc_ref, nch, k):
    rows = sc_ref.shape[1]
    kf = float(k)
    bcast = lambda col: jnp.broadcast_to(col, (rows, LANES))
    zero = jnp.zeros((rows, LANES), F32)
    key0 = jnp.where(bcast(_count(sc_ref, nch, lambda x: x >= zero)) >= kf, 0, INT_MIN).astype(I32)

    def body(it, key):
        cand = key | lax.shift_left(jnp.int32(1), jnp.asarray(30 - it).astype(I32))
        cand_f = _key_to_float(cand)
        tot = bcast(_count(sc_ref, nch, lambda x: x >= cand_f))
        return jnp.where((tot >= kf) | (cand <= KEY_NEG_INF), cand, key)

    key = lax.fori_loop(0, 31, body, key0)
    t_next = _key_to_float(key + 1)
    need = kf - _count(sc_ref, nch, lambda x: x >= t_next)
    return _key_to_float(key)[:, :1], t_next[:, :1], need


def _select_chunk(x, t, t_next, need, carry, tri):
    above = x >= t_next
    tie = (x >= t) & jnp.logical_not(above) & (x > NEG_INF)
    tief = jnp.where(tie, 1.0, 0.0)
    rank = jnp.dot(tief.astype(BF16), tri, preferred_element_type=F32) + carry
    sel = above | (tie & (rank < need))
    return sel, carry + jnp.sum(tief, axis=1, keepdims=True)


def _softmax_update(m_ref, l_ref, acc_ref, s, vmat, v_is_transposed=False):
    m_prev = m_ref[...]
    m_new = jnp.maximum(m_prev, jnp.max(s, axis=1, keepdims=True))
    alpha = jnp.exp(m_prev - m_new)
    p = jnp.exp(s - m_new)
    l_ref[...] = alpha * l_ref[...] + jnp.sum(p, axis=1, keepdims=True)
    if v_is_transposed:
        pv = lax.dot_general(p.astype(BF16), vmat, _NT, preferred_element_type=F32)
    else:
        pv = jnp.dot(p.astype(BF16), vmat, preferred_element_type=F32)
    acc_ref[...] = alpha * acc_ref[...] + pv
    m_ref[...] = m_new


def _softmax_init(m_ref, l_ref, acc_ref):
    m_ref[...] = jnp.full(m_ref.shape, MASK, F32)
    l_ref[...] = jnp.zeros(l_ref.shape, F32)
    acc_ref[...] = jnp.zeros(acc_ref.shape, F32)


def _dsa_prompt_kernel(qi_ref, q_ref, wi_ref, ki_ref, k_ref, v_ref, tri_ref, o_ref,
                       sc_ref, m_ref, l_ref, acc_ref, *, tq, topk):
    q0 = pl.program_id(1) * tq
    nch = (q0 + tq + CK - 1) // CK
    scale = HD ** -0.5
    qi4 = _split_bf16(_split_heads_rows(qi_ref[0], F32))
    wi = wi_ref[0]
    wcols = [wi[:, h:h + 1] for h in range(HEADS)]
    rowpos = q0 + lax.broadcasted_iota(I32, (tq, CK), 0)
    colpos0 = lax.broadcasted_iota(I32, (tq, CK), 1)

    q3 = jnp.concatenate([qi4[0], qi4[1], qi4[0]], axis=1)

    def fill(c, carry):
        off = pl.multiple_of(c * CK, CK)
        k_hi, k_lo = _split_bf16(ki_ref[0, pl.ds(off, CK), :])
        k3 = jnp.concatenate([k_hi, k_hi, k_lo], axis=1)
        s = lax.dot_general(q3, k3, _NT, preferred_element_type=F32)
        sc = jnp.zeros((tq, CK), F32)
        for h in range(HEADS):
            sc = sc + jnp.maximum(s[h * tq:(h + 1) * tq] * scale, 0.0) * wcols[h]
        sc_ref[c] = jnp.where(colpos0 + off <= rowpos, sc, NEG_INF)
        return carry

    lax.fori_loop(0, nch, fill, 0)

    t, t_next, need = _kth_largest(sc_ref, nch, topk)

    q4 = _split_heads_rows(q_ref[0])
    _softmax_init(m_ref, l_ref, acc_ref)

    def attend(c, carry):
        off = pl.multiple_of(c * CK, CK)
        sel, carry = _select_chunk(sc_ref[c], t, t_next, need, carry, tri_ref[...])
        bias = jnp.where(sel, 0.0, MASK)
        kc = k_ref[0, pl.ds(off, CK), :].astype(BF16)
        vc = v_ref[0, pl.ds(off, CK), :].astype(BF16)
        s = lax.dot_general(q4, kc, _NT, preferred_element_type=F32) * scale + _tile_rows(bias, HEADS)
        _softmax_update(m_ref, l_ref, acc_ref, s, vc)
        return carry

    lax.fori_loop(0, nch, attend, jnp.zeros((tq, 1), F32))
    o = acc_ref[...] / l_ref[...]
    for h in range(HEADS):
        o_ref[0, :, h * HD:(h + 1) * HD] = o[h * tq:(h + 1) * tq]


def _dsa_prompt(um, awi, aki, ak, av, tri, *, tq):
    b, s, _ = um.shape
    topk = min(TOPK_MAX, s // 4)
    kern = functools.partial(_dsa_prompt_kernel, tq=tq, topk=topk)
    full = lambda w: pl.BlockSpec((1, s, w), lambda bi, i: (bi, 0, 0))
    return pl.pallas_call(
        kern, grid=(b, s // tq),
        in_specs=[pl.BlockSpec((1, tq, GW), lambda bi, i: (bi, i, 1)),
                  pl.BlockSpec((1, tq, GW), lambda bi, i: (bi, i, 0)),
                  pl.BlockSpec((1, tq, 128), lambda bi, i: (bi, i, 0)),
                  full(HD), full(HD), full(HD),
                  pl.BlockSpec((CK, CK), lambda bi, i: (0, 0))],
        out_specs=pl.BlockSpec((1, tq, GW), lambda bi, i: (bi, i, 0)),
        out_shape=jax.ShapeDtypeStruct((b, s, GW), F32),
        scratch_shapes=[pltpu.VMEM((s // CK, tq, CK), F32), pltpu.VMEM((HEADS * tq, 1), F32),
                        pltpu.VMEM((HEADS * tq, 1), F32), pltpu.VMEM((HEADS * tq, HD), F32)],
        compiler_params=_cparams(2), name="dsa_prompt",
    )(um, um, awi, aki, ak, av, tri)


def _mla_prompt_kernel(q_ref, kc_ref, wuv_ref, o_ref, m_ref, l_ref, acc_ref, *, tq, kv_rank, scale):
    q0 = pl.program_id(1) * tq
    nch = (q0 + tq + CK - 1) // CK
    q = q_ref[:, 0].reshape(HEADS * tq, q_ref.shape[-1]).astype(BF16)
    rowpos = q0 + lax.broadcasted_iota(I32, (tq, CK), 0)
    colpos0 = lax.broadcasted_iota(I32, (tq, CK), 1)
    _softmax_init(m_ref, l_ref, acc_ref)

    def body(c, carry):
        off = pl.multiple_of(c * CK, CK)
        kc = kc_ref[0, pl.ds(off, CK), :].astype(BF16)
        bias = jnp.where(colpos0 + off <= rowpos, 0.0, MASK)
        s = lax.dot_general(q, kc, _NT, preferred_element_type=F32) * scale + _tile_rows(bias, HEADS)
        _softmax_update(m_ref, l_ref, acc_ref, s, kc[:, :kv_rank])
        return carry

    lax.fori_loop(0, nch, body, 0)
    o = acc_ref[...] / l_ref[...]
    for h in range(HEADS):
        oh = jnp.dot(o[h * tq:(h + 1) * tq].astype(BF16), wuv_ref[h], preferred_element_type=F32)
        o_ref[0, :, h * HD:(h + 1) * HD] = oh


def _mla_prompt(qcat, kcat, wuv, *, tq, scale):
    _, b, s, w = qcat.shape
    kv_rank = wuv.shape[1]
    kern = functools.partial(_mla_prompt_kernel, tq=tq, kv_rank=kv_rank, scale=scale)
    return pl.pallas_call(
        kern, grid=(b, s // tq),
        in_specs=[pl.BlockSpec((HEADS, 1, tq, w), lambda bi, i: (0, bi, i, 0)),
                  pl.BlockSpec((1, s, w), lambda bi, i: (bi, 0, 0)),
                  pl.BlockSpec((HEADS, kv_rank, HD), lambda bi, i: (0, 0, 0))],
        out_specs=pl.BlockSpec((1, tq, GW), lambda bi, i: (bi, i, 0)),
        out_shape=jax.ShapeDtypeStruct((b, s, GW), F32),
        scratch_shapes=[pltpu.VMEM((HEADS * tq, 1), F32), pltpu.VMEM((HEADS * tq, 1), F32),
                        pltpu.VMEM((HEADS * tq, kv_rank), F32)],
        compiler_params=_cparams(2), name="mla_prompt",
    )(qcat, kcat, wuv)


def _rglru_kernel(x_ref, h0_ref, c0_ref, cw_ref, cb_ref, wra_ref, bra_ref, wri_ref, bri_ref, lam_ref,
                  o_ref, hl_ref, xbuf, a_s, u_s, hst, *, tc, pad):
    @pl.when(pl.program_id(1) == 0)
    def _():
        hst[...] = h0_ref[0]
        xbuf[5:8, :] = c0_ref[0]

    x = x_ref[0]
    xbuf[8:8 + tc, :] = x
    cw = cw_ref[...]
    y = (cb_ref[...] + cw[0:1] * xbuf[5:5 + tc, :] + cw[1:2] * xbuf[6:6 + tc, :]
         + cw[2:3] * xbuf[7:7 + tc, :] + cw[3:4] * x)
    xbuf[5:8, :] = x[tc - 3:tc, :]

    yb = y.astype(BF16)
    r = jax.nn.sigmoid(jnp.dot(yb, wra_ref[...], preferred_element_type=F32) + bra_ref[...])
    ig = jax.nn.sigmoid(jnp.dot(yb, wri_ref[...], preferred_element_type=F32) + bri_ref[...])
    z = -lam_ref[...]
    softplus = jnp.maximum(z, 0.0) + jnp.log1p(jnp.exp(-jnp.abs(z)))
    log_a = -LRU_C * r * softplus
    a = jnp.exp(log_a)
    u = jnp.sqrt(1.0 - a * a) * ig * y

    w = x.shape[1]
    a_s[0:pad, :] = jnp.ones((pad, w), F32)
    u_s[0:pad, :] = jnp.zeros((pad, w), F32)
    a_s[pad:pad + tc, :] = a
    u_s[pad:pad + tc, :] = u
    sh = 1
    while sh < tc:
        a_c = a_s[pad:pad + tc, :]
        u_c = u_s[pad:pad + tc, :]
        a_p = a_s[pad - sh:pad - sh + tc, :]
        u_p = u_s[pad - sh:pad - sh + tc, :]
        u_s[pad:pad + tc, :] = a_c * u_p + u_c
        a_s[pad:pad + tc, :] = a_c * a_p
        sh *= 2
    hs = u_s[pad:pad + tc, :] + a_s[pad:pad + tc, :] * hst[...]
    o_ref[0] = hs
    hst[...] = hs[tc - 1:tc, :]
    hl_ref[0] = hs[tc - 1:tc, :]


def _rglru(bx_src, col, h0, c0, wts, *, tc):
    b, t, _ = bx_src.shape
    w = GW
    pad = max(8, tc // 2)
    kern = functools.partial(_rglru_kernel, tc=tc, pad=pad)
    const = lambda shape: pl.BlockSpec(shape, lambda bi, j: (0,) * len(shape))
    return pl.pallas_call(
        kern, grid=(b, t // tc),
        in_specs=[pl.BlockSpec((1, tc, w), lambda bi, j: (bi, j, col)),
                  pl.BlockSpec((1, 1, w), lambda bi, j: (bi, 0, 0)),
                  pl.BlockSpec((1, 3, w), lambda bi, j: (bi, 0, 0)),
                  const((4, w)), const((1, w)), const((w, w)), const((1, w)), const((w, w)),
                  const((1, w)), const((1, w))],
        out_specs=(pl.BlockSpec((1, tc, w), lambda bi, j: (bi, j, 0)),
                   pl.BlockSpec((1, 1, w), lambda bi, j: (bi, 0, 0))),
        out_shape=(jax.ShapeDtypeStruct((b, t, w), F32), jax.ShapeDtypeStruct((b, 1, w), F32)),
        scratch_shapes=[pltpu.VMEM((8 + tc, w), F32), pltpu.VMEM((pad + tc, w), F32),
                        pltpu.VMEM((pad + tc, w), F32), pltpu.VMEM((1, w), F32)],
        compiler_params=_cparams(2), name="rglru",
    )(bx_src, h0.reshape(b, 1, w), c0, wts["conv_w"], wts["conv_b"], wts["w_ra_bd"], wts["b_ra"],
      wts["w_ri_bd"], wts["b_ri"], wts["lam"])


def _matmul_kernel(x_ref, w_ref, o_ref):
    o_ref[...] = jnp.dot(x_ref[...].astype(BF16), w_ref[...], preferred_element_type=F32)


def _matmul(x, w, *, tm):
    r, d = x.shape
    n = w.shape[1]
    return pl.pallas_call(
        _matmul_kernel, grid=(r // tm,),
        in_specs=[pl.BlockSpec((tm, d), lambda i: (i, 0)), pl.BlockSpec((d, n), lambda i: (0, 0))],
        out_specs=pl.BlockSpec((tm, n), lambda i: (i, 0)),
        out_shape=jax.ShapeDtypeStruct((r, n), F32), compiler_params=_cparams(1), name="mem_kv",
    )(x, w)


def _mem_attn_kernel(q_ref, k_ref, v_ref, o_ref):
    q, k, v = q_ref[0], k_ref[0], v_ref[0]
    scale = HD ** -0.5
    for h in range(HEADS):
        sl = slice(h * HD, (h + 1) * HD)
        s = lax.dot_general(q[:, sl].astype(BF16), k[:, sl].astype(BF16), _NT,
                            preferred_element_type=F32) * scale
        p = jnp.exp(s - jnp.max(s, axis=1, keepdims=True))
        o = jnp.dot(p.astype(BF16), v[:, sl].astype(BF16), preferred_element_type=F32)
        o_ref[0, :, sl] = o / jnp.sum(p, axis=1, keepdims=True)


def _mem_attn(q_src, qcol, k_src, kcol, v_src, vcol, *, tq):
    b, t, _ = q_src.shape
    n = k_src.shape[1]
    return pl.pallas_call(
        _mem_attn_kernel, grid=(b, t // tq),
        in_specs=[pl.BlockSpec((1, tq, GW), lambda bi, i: (bi, i, qcol)),
                  pl.BlockSpec((1, n, GW), lambda bi, i: (bi, 0, kcol)),
                  pl.BlockSpec((1, n, GW), lambda bi, i: (bi, 0, vcol))],
        out_specs=pl.BlockSpec((1, tq, GW), lambda bi, i: (bi, i, 0)),
        out_shape=jax.ShapeDtypeStruct((b, t, GW), F32), compiler_params=_cparams(2), name="mem_attn",
    )(q_src, k_src, v_src)


def _mem_attn_t_kernel(q_ref, kt_ref, vt_ref, o_ref):
    q, kt, vt = q_ref[0], kt_ref[0], vt_ref[0]
    scale = HD ** -0.5
    for h in range(HEADS):
        sl = slice(h * HD, (h + 1) * HD)
        s = jnp.dot(q[:, sl].astype(BF16), kt[sl, :].astype(BF16), preferred_element_type=F32) * scale
        p = jnp.exp(s - jnp.max(s, axis=1, keepdims=True))
        o = lax.dot_general(p.astype(BF16), vt[sl, :].astype(BF16), _NT, preferred_element_type=F32)
        o_ref[0, :, sl] = o / jnp.sum(p, axis=1, keepdims=True)


def _mem_attn_t(layer, q_src, qcol, k_t, v_t):
    b, t, _ = q_src.shape
    n = k_t.shape[-1]
    cache = pl.BlockSpec((None, 1, GW, n), lambda bi: (layer, bi, 0, 0))
    return pl.pallas_call(
        _mem_attn_t_kernel, grid=(b,),
        in_specs=[pl.BlockSpec((1, t, GW), lambda bi: (bi, 0, qcol)), cache, cache],
        out_specs=pl.BlockSpec((1, t, GW), lambda bi: (bi, 0, 0)),
        out_shape=jax.ShapeDtypeStruct((b, t, GW), F32), compiler_params=_cparams(1), name="mem_attn_t",
    )(q_src, k_t, v_t)


def _out_proj_kernel(x_ref, ga_ref, gb_ref, gc_ref, gm_ref, oa_ref, ob_ref, oc_ref, om_ref,
                     w_ref, gpost_ref, y_ref):
    acc = None
    for idx, (o_r, g_r) in enumerate(((oa_ref, ga_ref), (ob_ref, gb_ref), (oc_ref, gc_ref), (om_ref, gm_ref))):
        g = g_r[...]
        part = (o_r[...] * (g * jax.nn.sigmoid(g))).astype(BF16)
        d = jnp.dot(part, w_ref[idx * GW:(idx + 1) * GW, :], preferred_element_type=F32)
        acc = d if acc is None else acc + d
    yn = acc * lax.rsqrt(jnp.mean(acc * acc, axis=-1, keepdims=True) + EPS) * gpost_ref[...]
    y_ref[...] = x_ref[...] + yn


def _out_proj(x2d, um, oa, ob, oc, om, w_out, g_post, *, tm):
    r, d = x2d.shape
    gate = lambda col: pl.BlockSpec((tm, GW), lambda i: (i, col))
    grp = pl.BlockSpec((tm, GW), lambda i: (i, 0))
    return pl.pallas_call(
        _out_proj_kernel, grid=(r // tm,),
        in_specs=[pl.BlockSpec((tm, d), lambda i: (i, 0)), gate(2), gate(4), gate(5), gate(7),
                  grp, grp, grp, grp,
                  pl.BlockSpec((4 * GW, d), lambda i: (0, 0)), pl.BlockSpec((1, d), lambda i: (0, 0))],
        out_specs=pl.BlockSpec((tm, d), lambda i: (i, 0)),
        out_shape=jax.ShapeDtypeStruct((r, d), F32), compiler_params=_cparams(1), name="out_proj",
    )(x2d, um, um, um, um, oa, ob, oc, om, w_out, g_post)


def _paged_fetch(pt_ref, layer, caches, bufs, sem, n_pages_step):
    s, g = pl.program_id(0), pl.program_id(1)
    ns, ng = pl.num_programs(0), pl.num_programs(1)
    n = s * ng + g
    slot = lax.rem(n, 2)

    def copy(a, page, slot_, j):
        return pltpu.make_async_copy(caches[a].at[layer, page], bufs[a].at[slot_, j], sem.at[slot_, a])

    def start(seq, grp, slot_):
        def body(j, carry):
            page = pt_ref[seq, grp * n_pages_step + j]
            for a in range(len(caches)):
                copy(a, page, slot_, j).start()
            return carry
        lax.fori_loop(0, n_pages_step, body, 0)

    @pl.when(n == 0)
    def _():
        start(0, 0, 0)

    @pl.when(n + 1 < ns * ng)
    def _():
        wraps = g + 1 == ng
        start(jnp.where(wraps, s + 1, s), jnp.where(wraps, 0, g + 1), 1 - slot)

    def wait_body(j, carry):
        for a in range(len(caches)):
            copy(a, 0, slot, j).wait()
        return carry
    lax.fori_loop(0, n_pages_step, wait_body, 0)
    return slot


def _sample_idx_kernel(pt_ref, qi_ref, wi_ref, kin_ref, cache_ref, scp_ref, scn_ref, kbuf, sem,
                       *, layer, n_pages_step):
    slot = _paged_fetch(pt_ref, layer, (cache_ref,), (kbuf,), sem, n_pages_step)
    scale = HD ** -0.5
    t = qi_ref.shape[1]
    qi4 = _split_bf16(_split_heads_rows(qi_ref[0], F32))
    wi = wi_ref[0]
    w_rows = jnp.concatenate([wi[:, h:h + 1] for h in range(HEADS)], axis=0)

    def score(s):
        r = jnp.maximum(s * scale, 0.0) * w_rows
        out = r[0:t]
        for h in range(1, HEADS):
            out = out + r[h * t:(h + 1) * t]
        return out

    kp = _split_bf16(jnp.concatenate([kbuf[slot, j] for j in range(n_pages_step)], axis=1))
    q3, k3 = _stack3(qi4, kp, 1, 0)
    scp_ref[0] = score(jnp.dot(q3, k3, preferred_element_type=F32))

    @pl.when(pl.program_id(1) == pl.num_programs(1) - 1)
    def _():
        kn = _split_bf16(jnp.concatenate([kin_ref[0], jnp.zeros((CK - t, HD), F32)], axis=0))
        sn = score(_dot3(qi4, kn, _NT))
        col = lax.broadcasted_iota(I32, (t, CK), 1)
        row = lax.broadcasted_iota(I32, (t, CK), 0)
        scn_ref[0] = jnp.where(col <= row, sn, -jnp.inf)


def _page_buffer(cache, n):
    return pltpu.VMEM((2, n) + tuple(cache.shape[2:]), cache.dtype)


def _sample_idx(layer, page_table, um_s, awi_s, aki_s, cache_idx_t):
    ns, t, _ = um_s.shape
    n_pages = page_table.shape[1]
    page = cache_idx_t.shape[3]
    p = min(IDX_PAGES_PER_STEP, n_pages)
    kern = functools.partial(_sample_idx_kernel, layer=layer, n_pages_step=p)
    grid_spec = pltpu.PrefetchScalarGridSpec(
        num_scalar_prefetch=1, grid=(ns, n_pages // p),
        in_specs=[pl.BlockSpec((1, t, GW), lambda s, g, pt: (s, 0, 1)),
                  pl.BlockSpec((1, t, 128), lambda s, g, pt: (s, 0, 0)),
                  pl.BlockSpec((1, t, HD), lambda s, g, pt: (s, 0, 0)),
                  pl.BlockSpec(memory_space=pl.ANY)],
        out_specs=(pl.BlockSpec((1, t, p * page), lambda s, g, pt: (s, 0, g)),
                   pl.BlockSpec((1, t, CK), lambda s, g, pt: (s, 0, 0))),
        scratch_shapes=[_page_buffer(cache_idx_t, p), pltpu.SemaphoreType.DMA((2, 1))])
    return pl.pallas_call(
        kern, grid_spec=grid_spec,
        out_shape=(jax.ShapeDtypeStruct((ns, t, n_pages * page), F32),
                   jax.ShapeDtypeStruct((ns, t, CK), F32)),
        compiler_params=_cparams(2), name="sample_idx",
    )(page_table, um_s, awi_s, aki_s, cache_idx_t)


def _topk_bias_kernel(scp_ref, scn_ref, tri_ref, bp_ref, bn_ref, sc_ref, *, topk):
    nchp = scp_ref.shape[1] // CK
    nch = nchp + 1
    for c in range(nchp):
        sc_ref[c] = scp_ref[:, c * CK:(c + 1) * CK]
    sc_ref[nchp] = scn_ref[...]

    t, t_next, need = _kth_largest(sc_ref, nch, topk)
    carry = jnp.zeros((sc_ref.shape[1], 1), F32)
    for c in range(nch):
        sel, carry = _select_chunk(sc_ref[c], t, t_next, need, carry, tri_ref[...])
        bias = jnp.where(sel, 0.0, MASK)
        if c < nchp:
            bp_ref[:, c * CK:(c + 1) * CK] = bias
        else:
            bn_ref[...] = bias


def _topk_bias(sc_past, sc_new, tri, *, topk, rb):
    nr, past = sc_past.shape
    kern = functools.partial(_topk_bias_kernel, topk=topk)
    return pl.pallas_call(
        kern, grid=(nr // rb,),
        in_specs=[pl.BlockSpec((rb, past), lambda i: (i, 0)), pl.BlockSpec((rb, CK), lambda i: (i, 0)),
                  pl.BlockSpec((CK, CK), lambda i: (0, 0))],
        out_specs=(pl.BlockSpec((rb, past), lambda i: (i, 0)), pl.BlockSpec((rb, CK), lambda i: (i, 0))),
        out_shape=(jax.ShapeDtypeStruct((nr, past), F32), jax.ShapeDtypeStruct((nr, CK), F32)),
        scratch_shapes=[pltpu.VMEM((past // CK + 1, rb, CK), F32)],
        compiler_params=_cparams(1), name="topk_bias",
    )(sc_past, sc_new, tri)


def _sample_attn_kernel(pt_ref, q_ref, qcat_ref, bp_ref, bn_ref, kn_ref, vn_ref, kcn_ref, et_ref, wuv_ref,
                        ck_ref, cv_ref, cc_ref, cr_ref, oa_ref, oc_ref,
                        kbuf, vbuf, cbuf, rbuf, sem, m1, l1, acc1, m2, l2, acc2,
                        *, layer, n_pages_step, kv_rank, scale_c):
    p = n_pages_step
    slot = _paged_fetch(pt_ref, layer, (ck_ref, cv_ref, cc_ref, cr_ref), (kbuf, vbuf, cbuf, rbuf), sem, p)
    g = pl.program_id(1)
    t = q_ref.shape[1]
    scale_a = HD ** -0.5

    @pl.when(g == 0)
    def _():
        _softmax_init(m1, l1, acc1)
        _softmax_init(m2, l2, acc2)

    q4 = _split_heads_rows(q_ref[0])
    qc = qcat_ref[:, 0].reshape(HEADS * t, qcat_ref.shape[-1])
    qcb = qc.astype(BF16)
    qlat = qcb[:, :kv_rank]
    qrope = jnp.dot(qcb[:, kv_rank:], et_ref[...], preferred_element_type=F32).astype(BF16)

    cat = lambda buf, axis: jnp.concatenate([buf[slot, j] for j in range(p)], axis=axis).astype(BF16)
    kp, vp, cp, rp = cat(kbuf, 1), cat(vbuf, 1), cat(cbuf, 0), cat(rbuf, 1)
    s1 = jnp.dot(q4, kp, preferred_element_type=F32) * scale_a + _tile_rows(bp_ref[0], HEADS)
    _softmax_update(m1, l1, acc1, s1, vp, v_is_transposed=True)
    s2 = (lax.dot_general(qlat, cp, _NT, preferred_element_type=F32)
          + jnp.dot(qrope, rp, preferred_element_type=F32)) * scale_c
    _softmax_update(m2, l2, acc2, s2, cp)

    @pl.when(g == pl.num_programs(1) - 1)
    def _():
        zpad = lambda a: jnp.concatenate([a, jnp.zeros((LANES - t, a.shape[1]), F32)], axis=0).astype(BF16)
        kn, vn, kcn = zpad(kn_ref[0]), zpad(vn_ref[0]), zpad(kcn_ref[0])
        s1n = (lax.dot_general(q4, kn, _NT, preferred_element_type=F32) * scale_a
               + _tile_rows(bn_ref[0][:, :LANES], HEADS))
        _softmax_update(m1, l1, acc1, s1n, vn)
        col = lax.broadcasted_iota(I32, (t, LANES), 1)
        row = lax.broadcasted_iota(I32, (t, LANES), 0)
        causal = jnp.where(col <= row, 0.0, MASK)
        s2n = lax.dot_general(qcb, kcn, _NT, preferred_element_type=F32) * scale_c + _tile_rows(causal, HEADS)
        _softmax_update(m2, l2, acc2, s2n, kcn[:, :kv_rank])
        oa = acc1[...] / l1[...]
        oc = acc2[...] / l2[...]
        for h in range(HEADS):
            oa_ref[0, :, h * HD:(h + 1) * HD] = oa[h * t:(h + 1) * t]
            oc_ref[0, :, h * HD:(h + 1) * HD] = jnp.dot(oc[h * t:(h + 1) * t].astype(BF16), wuv_ref[h],
                                                        preferred_element_type=F32)


def _sample_attn(layer, page_table, um_s, qcat_s, bias_past, bias_new, ak_s, av_s, kcat_s, et, wuv,
                 cache_k_t, cache_v_t, cache_c, cache_r_t, *, scale_c):
    ns, t, _ = um_s.shape
    n_pages = page_table.shape[1]
    page = cache_c.shape[2]
    kv_rank = cache_c.shape[3]
    rope = cache_r_t.shape[2]
    w = qcat_s.shape[-1]
    p = min(ATTN_PAGES_PER_STEP, n_pages)
    kern = functools.partial(_sample_attn_kernel, layer=layer, n_pages_step=p, kv_rank=kv_rank,
                             scale_c=scale_c)
    hbm = pl.BlockSpec(memory_space=pl.ANY)
    seq = lambda width: pl.BlockSpec((1, t, width), lambda s, g, pt: (s, 0, 0))
    grid_spec = pltpu.PrefetchScalarGridSpec(
        num_scalar_prefetch=1, grid=(ns, n_pages // p),
        in_specs=[seq(GW),
                  pl.BlockSpec((HEADS, 1, t, w), lambda s, g, pt: (0, s, 0, 0)),
                  pl.BlockSpec((1, t, p * page), lambda s, g, pt: (s, 0, g)),
                  seq(CK), seq(HD), seq(HD), seq(w),
                  pl.BlockSpec((128, rope), lambda s, g, pt: (0, 0)),
                  pl.BlockSpec((HEADS, kv_rank, HD), lambda s, g, pt: (0, 0, 0)),
                  hbm, hbm, hbm, hbm],
        out_specs=(seq(GW), seq(GW)),
        scratch_shapes=[_page_buffer(cache_k_t, p), _page_buffer(cache_v_t, p), _page_buffer(cache_c, p),
                        _page_buffer(cache_r_t, p), pltpu.SemaphoreType.DMA((2, 4)),
                        pltpu.VMEM((HEADS * t, 1), F32), pltpu.VMEM((HEADS * t, 1), F32),
                        pltpu.VMEM((HEADS * t, HD), F32), pltpu.VMEM((HEADS * t, 1), F32),
                        pltpu.VMEM((HEADS * t, 1), F32), pltpu.VMEM((HEADS * t, kv_rank), F32)])
    return pl.pallas_call(
        kern, grid_spec=grid_spec,
        out_shape=(jax.ShapeDtypeStruct((ns, t, GW), F32), jax.ShapeDtypeStruct((ns, t, GW), F32)),
        compiler_params=_cparams(2), name="sample_attn",
    )(page_table, um_s, qcat_s, bias_past, bias_new, ak_s, av_s, kcat_s, et, wuv,
      cache_k_t, cache_v_t, cache_c, cache_r_t)


def _prep_layer(l, g_pre, g_post, w_in, w_out, conv_w, conv_b, w_ra, b_ra, w_ri, b_ri, lam,
                g_q, w_uq, g_kv, w_uk, w_uv, w_mk, w_mv):
    q_rank = g_q.shape[-1]
    kv_rank = g_kv.shape[-1]
    nope = w_uk.shape[-1]
    rope = w_uq.shape[-1] // HEADS - nope
    half = rope // 2
    assert GW == w_in.shape[1] // 4 and rope == 32 and nope == HD and q_rank <= 256 and kv_rank == 128
    widths = (GW, HD, HD, GW, HD, HEADS, GW, GW, GW, q_rank, kv_rank, rope, GW, GW, GW)
    assert sum(widths) == w_in.shape[-1]
    offs = np.concatenate([[0], np.cumsum(widths)])
    w = w_in[l]
    col = lambda i: w[:, offs[i]:offs[i + 1]]
    padc = lambda a, n: jnp.pad(a, ((0, 0), (0, n - a.shape[1])))
    aq, ak, av, aqi, aki, awi, ag, bx, bg, cdq, cdkv, ckr, cg, mq, mg = [col(i) for i in range(15)]
    kr_dup = jnp.concatenate([ckr[:, :half]] * HEADS + [ckr[:, half:]] * HEADS, axis=1)
    w_all = jnp.concatenate([aq, aqi, ag, bx, bg, cg, mq, mg, padc(ak, 128), padc(av, 128), padc(aki, 128),
                             padc(awi, 128), padc(cdq, 256), cdkv, kr_dup], axis=1)
    assert w_all.shape[1] == _NP
    w_idx = jnp.concatenate([aqi, padc(aki, 128), padc(awi, 128)], axis=1)
    w_idx_hi = w_idx.astype(BF16)
    w_idx_lo = (w_idx - w_idx_hi.astype(F32)).astype(BF16)

    wq = w_uq[l].reshape(q_rank, HEADS, nope + rope)
    wq2 = jnp.concatenate([wq[:, :, :nope].reshape(q_rank, HEADS * nope),
                           wq[:, :, nope:nope + half].reshape(q_rank, HEADS * half),
                           wq[:, :, nope + half:].reshape(q_rank, HEADS * half)], axis=1)
    wq2 = jnp.pad(wq2, ((0, 256 - q_rank), (0, 0)))
    wuk_bd = jnp.zeros((HEADS * nope, HEADS * kv_rank), F32)
    for h in range(HEADS):
        wuk_bd = wuk_bd.at[h * nope:(h + 1) * nope, h * kv_rank:(h + 1) * kv_rank].set(w_uk[l][:, h, :].T)

    def block_diag(wb):
        nb, bd, _ = wb.shape
        out = jnp.zeros((nb * bd, nb * bd), F32)
        for n in range(nb):
            out = out.at[n * bd:(n + 1) * bd, n * bd:(n + 1) * bd].set(wb[n])
        return out

    assert half == ROPE_HALF
    lane = np.arange(128)
    head_mask = np.zeros((8, 128), np.float32)
    for h in range(HEADS):
        head_mask[h] = ((lane % 64) // half == h)
    return {
        "q_rank": q_rank, "kv_rank": kv_rank,
        "g_pre": g_pre[l][None], "g_post": g_post[l][None],
        "w_in": w_all.astype(BF16), "w_idx_hi": w_idx_hi, "w_idx_lo": w_idx_lo,
        "g_q": jnp.pad(g_q[l], (0, 256 - q_rank))[None], "w_q2": wq2.astype(BF16),
        "w_uk_bd": wuk_bd.astype(BF16), "g_kv": g_kv[l][None],
        "head_mask": jnp.asarray(head_mask),
        "w_out": w_out[l].astype(BF16),
        "conv_w": conv_w[l], "conv_b": conv_b[l][None],
        "w_ra_bd": block_diag(w_ra[l]).astype(BF16), "b_ra": b_ra[l][None],
        "w_ri_bd": block_diag(w_ri[l]).astype(BF16), "b_ri": b_ri[l][None], "lam": lam[l][None],
        "w_uv": jnp.transpose(w_uv[l], (1, 0, 2)).astype(BF16),
        "w_mkv": jnp.concatenate([w_mk[l], w_mv[l]], axis=1).astype(BF16),
        "scale_c": float((nope + rope) ** -0.5),
    }


def kernel(x_prompt, x_sample, mem_prompt, cache_a_k, cache_a_v, cache_a_idx, cache_c_lat, cache_c_kr,
           state_b_h, state_b_conv, cache_mem_k, cache_mem_v, page_table,
           g_pre, g_post, w_in, w_out, conv_w, conv_b, w_ra, b_ra, w_ri, b_ri, lam,
           g_q, w_uq, g_kv, w_uk, w_uv, w_mk, w_mv):
    b, s, d = x_prompt.shape
    ns, t, _ = x_sample.shape
    depth = w_in.shape[0]
    n_mem = mem_prompt.shape[1]
    past = page_table.shape[1] * cache_a_k.shape[2]
    rope = cache_c_kr.shape[-1]
    n_pages = page_table.shape[1]
    assert s % CK == 0 and past % CK == 0 and t == 8
    assert n_pages % min(IDX_PAGES_PER_STEP, n_pages) == 0 and n_pages % min(ATTN_PAGES_PER_STEP, n_pages) == 0

    tri = jnp.asarray(np.triu(np.ones((CK, CK), np.float32), k=1), BF16)
    et_np = np.zeros((128, rope), np.float32)
    for lane in range(128):
        et_np[lane, (lane // 64) * (rope // 2) + lane % (rope // 2)] = 1.0
    et = jnp.asarray(et_np, BF16)

    cache_idx_t = jnp.swapaxes(cache_a_idx, 2, 3)
    cache_k_t = jnp.swapaxes(cache_a_k, 2, 3)
    cache_v_t = jnp.swapaxes(cache_a_v, 2, 3)
    cache_r_t = jnp.swapaxes(cache_c_kr, 2, 3)
    mem_k_t = jnp.transpose(cache_mem_k, (0, 1, 3, 4, 2)).reshape(depth, ns, GW, n_mem)
    mem_v_t = jnp.transpose(cache_mem_v, (0, 1, 3, 4, 2)).reshape(depth, ns, GW, n_mem)

    xp = x_prompt.reshape(b * s, d)
    xs = x_sample.reshape(ns * t, d)
    mem2d = mem_prompt.reshape(b * n_mem, d)
    p_states = [[] for _ in range(9)]
    s_states = [[] for _ in range(7)]
    for l in range(depth):
        wts = _prep_layer(l, g_pre, g_post, w_in, w_out, conv_w, conv_b, w_ra, b_ra, w_ri, b_ri, lam,
                          g_q, w_uq, g_kv, w_uk, w_uv, w_mk, w_mv)
        kv_rank = wts["kv_rank"]
        um, ak, av, aki, awi, c, kr, qcat, kcat = _in_proj(xp, wts, tm=512, pos_base=0, pos_mod=s)
        um3 = um.reshape(b, s, _UM)
        r3 = lambda a: a.reshape(b, s, a.shape[-1])
        oa = _dsa_prompt(um3, r3(awi), r3(aki), r3(ak), r3(av), tri, tq=128)
        oc = _mla_prompt(qcat.reshape(HEADS, b, s, kv_rank + 128), r3(kcat), wts["w_uv"], tq=128,
                         scale=wts["scale_c"])
        ob, bh = _rglru(um3, 3, jnp.zeros((b, GW), F32), jnp.zeros((b, 3, GW), F32), wts, tc=512)
        mkv = _matmul(mem2d, wts["w_mkv"], tm=256).reshape(b, n_mem, 2 * GW)
        om = _mem_attn(um3, 6, mkv, 0, mkv, 1, tq=256)
        flat = lambda a: a.reshape(b * s, GW)
        xp_new = _out_proj(xp, um, flat(oa), flat(ob), flat(oc), flat(om), wts["w_out"], wts["g_post"], tm=512)
        st_p = (r3(ak), r3(av), r3(aki), r3(c), r3(kr), bh.reshape(b, GW), um3[:, s - 3:, 3 * GW:4 * GW],
                mkv[:, :, :GW].reshape(b, n_mem, HEADS, HD), mkv[:, :, GW:].reshape(b, n_mem, HEADS, HD))
        um_s, ak_s, av_s, aki_s, awi_s, c_s, kr_s, qcat_s, kcat_s = _in_proj(
            xs, wts, tm=min(512, ns * t), pos_base=past, pos_mod=t)
        um_s3 = um_s.reshape(ns, t, _UM)
        q3 = lambda a: a.reshape(ns, t, a.shape[-1])
        sc_past, sc_new = _sample_idx(l, page_table, um_s3, q3(awi_s), q3(aki_s), cache_idx_t)
        topk = min(TOPK_MAX, (past + t) // 4)
        bias_past, bias_new = _topk_bias(sc_past.reshape(ns * t, past), sc_new.reshape(ns * t, CK), tri,
                                         topk=topk, rb=min(64, ns * t))
        oa_s, oc_s = _sample_attn(l, page_table, um_s3, qcat_s.reshape(HEADS, ns, t, kv_rank + 128),
                                  bias_past.reshape(ns, t, past), bias_new.reshape(ns, t, CK),
                                  q3(ak_s), q3(av_s), q3(kcat_s), et, wts["w_uv"],
                                  cache_k_t, cache_v_t, cache_c_lat, cache_r_t, scale_c=wts["scale_c"])
        ob_s, bh_s = _rglru(um_s3, 3, state_b_h[l], state_b_conv[l], wts, tc=t)
        om_s = _mem_attn_t(l, um_s3, 6, mem_k_t, mem_v_t)
        flat_s = lambda a: a.reshape(ns * t, GW)
        xs_new = _out_proj(xs, um_s, flat_s(oa_s), flat_s(ob_s), flat_s(oc_s), flat_s(om_s),
                           wts["w_out"], wts["g_post"], tm=min(512, ns * t))
        st_s = (q3(ak_s), q3(av_s), q3(aki_s), q3(c_s), q3(kr_s), bh_s.reshape(ns, GW),
                um_s3[:, t - 3:, 3 * GW:4 * GW])
        xp, xs = xp_new, xs_new
        for lst, v in zip(p_states, st_p):
            lst.append(v)
        for lst, v in zip(s_states, st_s):
            lst.append(v)
    outs_p = [jnp.stack(v) for v in p_states]
    outs_s = [jnp.stack(v) for v in s_states]
    return (xp.reshape(b, s, d), xs.reshape(ns, t, d), *outs_p, *outs_s)
```

```python
import functools

import numpy as np
import jax
import jax.numpy as jnp
from jax import lax
from jax.experimental import pallas as pl
from jax.experimental.pallas import tpu as pltpu

F32 = jnp.float32
BF16 = jnp.bfloat16
I32 = jnp.int32

EPS = 1e-6
LRU_C = 8.0
ROPE_BASE = 10000.0
ROPE_HALF = 16
_LN_BASE_HI = float(np.float32(np.log(ROPE_BASE)))
_LN_BASE_LO = float(np.float32(np.log(ROPE_BASE) - np.float64(np.float32(np.log(ROPE_BASE)))))
TOPK_MAX = 256
MASK = -1e30
NEG_INF = float("-inf")
INT_MIN = int(np.iinfo(np.int32).min)
KEY_NEG_INF = -2139095041

LANES = 128
HEADS = 4
HD = 64
GW = HEADS * HD
CK = 512
VMEM_LIMIT = 48 * 1024 * 1024
IDX_PAGES_PER_STEP = 128
ATTN_PAGES_PER_STEP = 64

_UM = 8 * GW
_O_AK, _O_AV, _O_AKI, _O_AWI = _UM, _UM + 128, _UM + 256, _UM + 384
_O_CDQ = _UM + 512
_O_CDKV = _O_CDQ + 256
_O_KR = _O_CDKV + 128
_NP = _O_KR + 128

_NT = (((1,), (1,)), ((), ()))


def _cparams(n_axes):
    return pltpu.CompilerParams(dimension_semantics=("arbitrary",) * n_axes,
                                vmem_limit_bytes=VMEM_LIMIT)


def _split_heads_rows(a, dtype=BF16):
    return jnp.concatenate([a[:, h * HD:(h + 1) * HD] for h in range(HEADS)], axis=0).astype(dtype)


def _tile_rows(a, n):
    return jnp.concatenate([a] * n, axis=0)


def _split_bf16(a):
    hi = a.astype(BF16)
    return hi, (a - hi.astype(F32)).astype(BF16)


def _dot3(a, b, dims):
    dot = lambda x, y: lax.dot_general(x, y, dims, preferred_element_type=F32)
    return dot(a[0], b[0]) + (dot(a[1], b[0]) + dot(a[0], b[1]))


def _stack3(a, b, a_axis, b_axis):
    return (jnp.concatenate([a[0], a[1], a[0]], axis=a_axis),
            jnp.concatenate([b[0], b[0], b[1]], axis=b_axis))


def _in_proj_kernel(x_ref, gpre_ref, w_ref, wih_ref, wil_ref, gq_ref, wq2_ref, wuk_ref, gkv_ref, hm_ref,
                    um_ref, ak_ref, av_ref, aki_ref, awi_ref, c_ref, kr_ref, qcat_ref, kcat_ref,
                    *, tm, pos_base, pos_mod, q_rank, kv_rank, idx_scale):
    i = pl.program_id(0)
    x = x_ref[...]
    h = x * lax.rsqrt(jnp.mean(x * x, axis=-1, keepdims=True) + EPS) * gpre_ref[...]
    hs = _split_bf16(h)
    u = jnp.dot(hs[0], w_ref[...], preferred_element_type=F32)
    um_ref[...] = u[:, :_UM]
    ak_ref[...] = u[:, _O_AK:_O_AK + HD]
    av_ref[...] = u[:, _O_AV:_O_AV + HD]
    ui = _dot3(hs, (wih_ref[...], wil_ref[...]), (((1,), (0,)), ((), ())))
    um_ref[:, GW:2 * GW] = ui[:, :GW]
    aki_ref[...] = ui[:, GW:GW + HD]
    awi_ref[...] = ui[:, GW + 128:GW + 256] * idx_scale

    cdq = u[:, _O_CDQ:_O_CDQ + 256]
    cq = cdq * lax.rsqrt(jnp.sum(cdq * cdq, axis=-1, keepdims=True) * (1.0 / q_rank) + EPS) * gq_ref[...]
    qf = jnp.dot(cq.astype(BF16), wq2_ref[...], preferred_element_type=F32)
    qlat = jnp.dot(qf[:, :GW].astype(BF16), wuk_ref[...], preferred_element_type=F32)

    cdkv = u[:, _O_CDKV:_O_CDKV + kv_rank]
    c = cdkv * lax.rsqrt(jnp.mean(cdkv * cdkv, axis=-1, keepdims=True) + EPS) * gkv_ref[...]
    c_ref[...] = c

    row = i * tm + lax.broadcasted_iota(I32, (tm, 1), 0)
    pos = (pos_base + lax.rem(row, pos_mod)).astype(F32)
    j = lax.rem(lax.broadcasted_iota(I32, (1, 64), 1), ROPE_HALF).astype(F32)
    e = -(j / ROPE_HALF)
    freqs = jnp.exp(e * _LN_BASE_HI + e * _LN_BASE_LO)
    ang = pos * freqs
    cos = jnp.cos(ang)
    sin = jnp.sin(ang)

    def rot(a):
        x1, x2 = a[:, :64], a[:, 64:]
        return x1 * cos - x2 * sin, x1 * sin + x2 * cos

    q1, q2 = rot(qf[:, GW:GW + 128])
    k1, k2 = rot(u[:, _O_KR:_O_KR + 128])
    kr_ref[...] = jnp.concatenate([k1[:, :16], k2[:, :16]], axis=1)
    kcat_ref[:, 0:kv_rank] = c
    kcat_ref[:, kv_rank:kv_rank + 64] = k1
    kcat_ref[:, kv_rank + 64:kv_rank + 128] = k2
    qr = jnp.concatenate([q1, q2], axis=1)
    for hh in range(HEADS):
        qcat_ref[hh, :, 0:kv_rank] = qlat[:, hh * kv_rank:(hh + 1) * kv_rank]
        qcat_ref[hh, :, kv_rank:kv_rank + 128] = qr * hm_ref[hh:hh + 1, :]


def _in_proj(x2d, wts, *, tm, pos_base, pos_mod):
    r, d = x2d.shape
    kv_rank = wts["kv_rank"]
    kern = functools.partial(_in_proj_kernel, tm=tm, pos_base=pos_base, pos_mod=pos_mod,
                             q_rank=wts["q_rank"], kv_rank=kv_rank, idx_scale=HEADS ** -0.5)
    const = lambda shape: pl.BlockSpec(shape, lambda i: (0,) * len(shape))
    rows = lambda w: pl.BlockSpec((tm, w), lambda i: (i, 0))
    out_shape = (
        jax.ShapeDtypeStruct((r, _UM), F32),
        jax.ShapeDtypeStruct((r, HD), F32), jax.ShapeDtypeStruct((r, HD), F32),
        jax.ShapeDtypeStruct((r, HD), F32), jax.ShapeDtypeStruct((r, 128), F32),
        jax.ShapeDtypeStruct((r, kv_rank), F32), jax.ShapeDtypeStruct((r, 32), F32),
        jax.ShapeDtypeStruct((HEADS, r, kv_rank + 128), F32),
        jax.ShapeDtypeStruct((r, kv_rank + 128), F32),
    )
    out_specs = (rows(_UM), rows(HD), rows(HD), rows(HD), rows(128), rows(kv_rank), rows(32),
                 pl.BlockSpec((HEADS, tm, kv_rank + 128), lambda i: (0, i, 0)), rows(kv_rank + 128))
    return pl.pallas_call(
        kern, grid=(r // tm,),
        in_specs=[rows(d), const((1, d)), const((d, _NP)), const((d, GW + 256)), const((d, GW + 256)),
                  const((1, 256)), const((256, GW + 128)),
                  const((GW, HEADS * kv_rank)), const((1, kv_rank)), const((8, 128))],
        out_specs=out_specs, out_shape=out_shape, compiler_params=_cparams(1), name="in_proj",
    )(x2d, wts["g_pre"], wts["w_in"], wts["w_idx_hi"], wts["w_idx_lo"], wts["g_q"], wts["w_q2"],
      wts["w_uk_bd"], wts["g_kv"], wts["head_mask"])


def _key_to_float(key):
    bits = key ^ (lax.shift_right_arithmetic(key, 31) & 0x7FFFFFFF)
    return lax.bitcast_convert_type(bits, F32)


def _count(sc_ref, nch, pred):
    rows, ck = sc_ref.shape[1], sc_ref.shape[2]

    def body(c, acc):
        x = sc_ref[c]
        for j in range(ck // LANES):
            acc = acc + jnp.where(pred(x[:, j * LANES:(j + 1) * LANES]), 1.0, 0.0)
        return acc

    acc = lax.fori_loop(0, nch, body, jnp.zeros((rows, LANES), F32))
    return jnp.sum(acc, axis=1, keepdims=True)


def _kth_largest(sc_ref, nch, k):
    rows = sc_ref.shape[1]
    kf = float(k)
    bcast = lambda col: jnp.broadcast_to(col, (rows, LANES))
    zero = jnp.zeros((rows, LANES), F32)
    key0 = jnp.where(bcast(_count(sc_ref, nch, lambda x: x >= zero)) >= kf, 0, INT_MIN).astype(I32)

    def body(it, key):
        cand = key | lax.shift_left(jnp.int32(1), jnp.asarray(30 - it).astype(I32))
        cand_f = _key_to_float(cand)
        tot = bcast(_count(sc_ref, nch, lambda x: x >= cand_f))
        return jnp.where((tot >= kf) | (cand <= KEY_NEG_INF), cand, key)

    key = lax.fori_loop(0, 31, body, key0)
    t_next = _key_to_float(key + 1)
    need = kf - _count(sc_ref, nch, lambda x: x >= t_next)
    return _key_to_float(key)[:, :1], t_next[:, :1], need


def _select_chunk(x, t, t_next, need, carry, tri):
    above = x >= t_next
    tie = (x >= t) & jnp.logical_not(above) & (x > NEG_INF)
    tief = jnp.where(tie, 1.0, 0.0)
    rank = jnp.dot(tief.astype(BF16), tri, preferred_element_type=F32) + carry
    sel = above | (tie & (rank < need))
    return sel, carry + jnp.sum(tief, axis=1, keepdims=True)


def _softmax_update(m_ref, l_ref, acc_ref, s, vmat, v_is_transposed=False):
    m_prev = m_ref[...]
    m_new = jnp.maximum(m_prev, jnp.max(s, axis=1, keepdims=True))
    alpha = jnp.exp(m_prev - m_new)
    p = jnp.exp(s - m_new)
    l_ref[...] = alpha * l_ref[...] + jnp.sum(p, axis=1, keepdims=True)
    if v_is_transposed:
        pv = lax.dot_general(p.astype(BF16), vmat, _NT, preferred_element_type=F32)
    else:
        pv = jnp.dot(p.astype(BF16), vmat, preferred_element_type=F32)
    acc_ref[...] = alpha * acc_ref[...] + pv
    m_ref[...] = m_new


def _softmax_init(m_ref, l_ref, acc_ref):
    m_ref[...] = jnp.full(m_ref.shape, MASK, F32)
    l_ref[...] = jnp.zeros(l_ref.shape, F32)
    acc_ref[...] = jnp.zeros(acc_ref.shape, F32)


def _dsa_prompt_kernel(qi_ref, q_ref, wi_ref, ki_ref, k_ref, v_ref, tri_ref, o_ref,
                       sc_ref, m_ref, acc_ref, s_ref, *, tq, topk):
    q0 = pl.program_id(1) * tq
    nch = (q0 + tq + CK - 1) // CK
    scale = HD ** -0.5
    qi4 = _split_bf16(_split_heads_rows(qi_ref[0], F32))
    wi = wi_ref[0]
    wcols = [wi[:, h:h + 1] for h in range(HEADS)]
    rowpos = q0 + lax.broadcasted_iota(I32, (tq, CK), 0)
    colpos0 = lax.broadcasted_iota(I32, (tq, CK), 1)

    q3 = jnp.concatenate([qi4[0], qi4[1], qi4[0]], axis=1)

    def fill(c, carry):
        off = pl.multiple_of(c * CK, CK)
        k_hi, k_lo = _split_bf16(ki_ref[0, pl.ds(off, CK), :])
        k3 = jnp.concatenate([k_hi, k_hi, k_lo], axis=1)
        s = lax.dot_general(q3, k3, _NT, preferred_element_type=F32)
        sc = jnp.zeros((tq, CK), F32)
        for h in range(HEADS):
            sc = sc + jnp.maximum(s[h * tq:(h + 1) * tq] * scale, 0.0) * wcols[h]
        sc_ref[c] = jnp.where(colpos0 + off <= rowpos, sc, NEG_INF)
        return carry

    lax.fori_loop(0, nch, fill, 0)

    t, t_next, need = _kth_largest(sc_ref, nch, topk)

    q4 = _split_heads_rows(q_ref[0])
    ones = jnp.ones((CK, HD), BF16)
    m_ref[...] = jnp.full(m_ref.shape, MASK, F32)
    acc_ref[...] = jnp.zeros(acc_ref.shape, F32)

    def logits(c):
        off = pl.multiple_of(c * CK, CK)
        return lax.dot_general(q4, k_ref[0, pl.ds(off, CK), :].astype(BF16), _NT, preferred_element_type=F32)

    s_ref[0] = logits(0)

    def attend(c, carry):
        off = pl.multiple_of(c * CK, CK)
        sel, carry = _select_chunk(sc_ref[c], t, t_next, need, carry, tri_ref[...])
        bias = jnp.where(sel, 0.0, MASK)
        s = s_ref[lax.rem(c, 2)] * scale + _tile_rows(bias, HEADS)
        s_ref[lax.rem(c + 1, 2)] = logits(jnp.minimum(c + 1, nch - 1))
        vext = jnp.concatenate([v_ref[0, pl.ds(off, CK), :].astype(BF16), ones], axis=1)
        m_prev = m_ref[...]
        m_new = jnp.maximum(m_prev, jnp.max(s, axis=1, keepdims=True))
        p = jnp.exp(s - m_new)
        acc_ref[...] = (jnp.exp(m_prev - m_new) * acc_ref[...]
                        + jnp.dot(p.astype(BF16), vext, preferred_element_type=F32))
        m_ref[...] = m_new
        return carry

    lax.fori_loop(0, nch, attend, jnp.zeros((tq, 1), F32))
    acc = acc_ref[...]
    o = acc[:, :HD] / acc[:, HD:]
    for h in range(HEADS):
        o_ref[0, :, h * HD:(h + 1) * HD] = o[h * tq:(h + 1) * tq]


def _dsa_prompt(um, awi, aki, ak, av, tri, *, tq):
    b, s, _ = um.shape
    topk = min(TOPK_MAX, s // 4)
    kern = functools.partial(_dsa_prompt_kernel, tq=tq, topk=topk)
    full = lambda w: pl.BlockSpec((1, s, w), lambda bi, i: (bi, 0, 0))
    return pl.pallas_call(
        kern, grid=(b, s // tq),
        in_specs=[pl.BlockSpec((1, tq, GW), lambda bi, i: (bi, i, 1)),
                  pl.BlockSpec((1, tq, GW), lambda bi, i: (bi, i, 0)),
                  pl.BlockSpec((1, tq, 128), lambda bi, i: (bi, i, 0)),
                  full(HD), full(HD), full(HD),
                  pl.BlockSpec((CK, CK), lambda bi, i: (0, 0))],
        out_specs=pl.BlockSpec((1, tq, GW), lambda bi, i: (bi, i, 0)),
        out_shape=jax.ShapeDtypeStruct((b, s, GW), F32),
        scratch_shapes=[pltpu.VMEM((s // CK, tq, CK), F32), pltpu.VMEM((HEADS * tq, 1), F32),
                        pltpu.VMEM((HEADS * tq, 2 * HD), F32), pltpu.VMEM((2, HEADS * tq, CK), F32)],
        compiler_params=_cparams(2), name="dsa_prompt",
    )(um, um, awi, aki, ak, av, tri)


def _mla_prompt_kernel(q_ref, kc_ref, wuv_ref, o_ref, m_ref, acc_ref, s_ref, *, tq, kv_rank, scale):
    q0 = pl.program_id(1) * tq
    nch = (q0 + tq + CK - 1) // CK
    q = q_ref[:, 0].reshape(HEADS * tq, q_ref.shape[-1]).astype(BF16)
    rowpos = q0 + lax.broadcasted_iota(I32, (tq, CK), 0)
    colpos0 = lax.broadcasted_iota(I32, (tq, CK), 1)
    ones = jnp.ones((CK, LANES), BF16)
    m_ref[...] = jnp.full(m_ref.shape, MASK, F32)
    acc_ref[...] = jnp.zeros(acc_ref.shape, F32)

    def scores(c):
        off = pl.multiple_of(c * CK, CK)
        return lax.dot_general(q, kc_ref[0, pl.ds(off, CK), :].astype(BF16), _NT, preferred_element_type=F32)

    s_ref[0] = scores(0)

    def body(c, carry):
        off = pl.multiple_of(c * CK, CK)
        bias = jnp.where(colpos0 + off <= rowpos, 0.0, MASK)
        s = s_ref[lax.rem(c, 2)] * scale + _tile_rows(bias, HEADS)
        s_ref[lax.rem(c + 1, 2)] = scores(jnp.minimum(c + 1, nch - 1))
        vext = jnp.concatenate([kc_ref[0, pl.ds(off, CK), :kv_rank].astype(BF16), ones], axis=1)
        m_prev = m_ref[...]
        m_new = jnp.maximum(m_prev, jnp.max(s, axis=1, keepdims=True))
        p = jnp.exp(s - m_new)
        acc_ref[...] = (jnp.exp(m_prev - m_new) * acc_ref[...]
                        + jnp.dot(p.astype(BF16), vext, preferred_element_type=F32))
        m_ref[...] = m_new
        return carry

    lax.fori_loop(0, nch, body, 0)
    acc = acc_ref[...]
    o = acc[:, :kv_rank] / acc[:, kv_rank:]
    for h in range(HEADS):
        oh = jnp.dot(o[h * tq:(h + 1) * tq].astype(BF16), wuv_ref[h], preferred_element_type=F32)
        o_ref[0, :, h * HD:(h + 1) * HD] = oh


def _mla_prompt(qcat, kcat, wuv, *, tq, scale):
    _, b, s, w = qcat.shape
    kv_rank = wuv.shape[1]
    kern = functools.partial(_mla_prompt_kernel, tq=tq, kv_rank=kv_rank, scale=scale)
    return pl.pallas_call(
        kern, grid=(b, s // tq),
        in_specs=[pl.BlockSpec((HEADS, 1, tq, w), lambda bi, i: (0, bi, i, 0)),
                  pl.BlockSpec((1, s, w), lambda bi, i: (bi, 0, 0)),
                  pl.BlockSpec((HEADS, kv_rank, HD), lambda bi, i: (0, 0, 0))],
        out_specs=pl.BlockSpec((1, tq, GW), lambda bi, i: (bi, i, 0)),
        out_shape=jax.ShapeDtypeStruct((b, s, GW), F32),
        scratch_shapes=[pltpu.VMEM((HEADS * tq, 1), F32), pltpu.VMEM((HEADS * tq, kv_rank + LANES), F32),
                        pltpu.VMEM((2, HEADS * tq, CK), F32)],
        compiler_params=_cparams(2), name="mla_prompt",
    )(qcat, kcat, wuv)


def _rglru_kernel(x_ref, h0_ref, c0_ref, cw_ref, cb_ref, wra_ref, bra_ref, wri_ref, bri_ref, lam_ref,
                  o_ref, hl_ref, xbuf, a_s, u_s, hst, *, tc, pad):
    @pl.when(pl.program_id(1) == 0)
    def _():
        hst[...] = h0_ref[0]
        xbuf[5:8, :] = c0_ref[0]

    x = x_ref[0]
    xbuf[8:8 + tc, :] = x
    cw = cw_ref[...]
    y = (cb_ref[...] + cw[0:1] * xbuf[5:5 + tc, :] + cw[1:2] * xbuf[6:6 + tc, :]
         + cw[2:3] * xbuf[7:7 + tc, :] + cw[3:4] * x)
    xbuf[5:8, :] = x[tc - 3:tc, :]

    yb = y.astype(BF16)
    r = jax.nn.sigmoid(jnp.dot(yb, wra_ref[...], preferred_element_type=F32) + bra_ref[...])
    ig = jax.nn.sigmoid(jnp.dot(yb, wri_ref[...], preferred_element_type=F32) + bri_ref[...])
    z = -lam_ref[...]
    softplus = jnp.maximum(z, 0.0) + jnp.log1p(jnp.exp(-jnp.abs(z)))
    log_a = -LRU_C * r * softplus
    a = jnp.exp(log_a)
    u = jnp.sqrt(1.0 - a * a) * ig * y

    w = x.shape[1]
    a_s[0:pad, :] = jnp.ones((pad, w), F32)
    u_s[0:pad, :] = jnp.zeros((pad, w), F32)
    a_s[pad:pad + tc, :] = a
    u_s[pad:pad + tc, :] = u
    sh = 1
    while sh < tc:
        a_c = a_s[pad:pad + tc, :]
        u_c = u_s[pad:pad + tc, :]
        a_p = a_s[pad - sh:pad - sh + tc, :]
        u_p = u_s[pad - sh:pad - sh + tc, :]
        u_s[pad:pad + tc, :] = a_c * u_p + u_c
        a_s[pad:pad + tc, :] = a_c * a_p
        sh *= 2
    hs = u_s[pad:pad + tc, :] + a_s[pad:pad + tc, :] * hst[...]
    o_ref[0] = hs
    hst[...] = hs[tc - 1:tc, :]
    hl_ref[0] = hs[tc - 1:tc, :]


def _rglru(bx_src, col, h0, c0, wts, *, tc):
    b, t, _ = bx_src.shape
    w = GW
    pad = max(8, tc // 2)
    kern = functools.partial(_rglru_kernel, tc=tc, pad=pad)
    const = lambda shape: pl.BlockSpec(shape, lambda bi, j: (0,) * len(shape))
    return pl.pallas_call(
        kern, grid=(b, t // tc),
        in_specs=[pl.BlockSpec((1, tc, w), lambda bi, j: (bi, j, col)),
                  pl.BlockSpec((1, 1, w), lambda bi, j: (bi, 0, 0)),
                  pl.BlockSpec((1, 3, w), lambda bi, j: (bi, 0, 0)),
                  const((4, w)), const((1, w)), const((w, w)), const((1, w)), const((w, w)),
                  const((1, w)), const((1, w))],
        out_specs=(pl.BlockSpec((1, tc, w), lambda bi, j: (bi, j, 0)),
                   pl.BlockSpec((1, 1, w), lambda bi, j: (bi, 0, 0))),
        out_shape=(jax.ShapeDtypeStruct((b, t, w), F32), jax.ShapeDtypeStruct((b, 1, w), F32)),
        scratch_shapes=[pltpu.VMEM((8 + tc, w), F32), pltpu.VMEM((pad + tc, w), F32),
                        pltpu.VMEM((pad + tc, w), F32), pltpu.VMEM((1, w), F32)],
        compiler_params=_cparams(2), name="rglru",
    )(bx_src, h0.reshape(b, 1, w), c0, wts["conv_w"], wts["conv_b"], wts["w_ra_bd"], wts["b_ra"],
      wts["w_ri_bd"], wts["b_ri"], wts["lam"])


def _matmul_kernel(x_ref, w_ref, o_ref):
    o_ref[...] = jnp.dot(x_ref[...].astype(BF16), w_ref[...], preferred_element_type=F32)


def _matmul(x, w, *, tm):
    r, d = x.shape
    n = w.shape[1]
    return pl.pallas_call(
        _matmul_kernel, grid=(r // tm,),
        in_specs=[pl.BlockSpec((tm, d), lambda i: (i, 0)), pl.BlockSpec((d, n), lambda i: (0, 0))],
        out_specs=pl.BlockSpec((tm, n), lambda i: (i, 0)),
        out_shape=jax.ShapeDtypeStruct((r, n), F32), compiler_params=_cparams(1), name="mem_kv",
    )(x, w)


def _mem_attn_kernel(q_ref, k_ref, v_ref, o_ref):
    q, k, v = q_ref[0], k_ref[0], v_ref[0]
    scale = HD ** -0.5
    for h in range(HEADS):
        sl = slice(h * HD, (h + 1) * HD)
        s = lax.dot_general(q[:, sl].astype(BF16), k[:, sl].astype(BF16), _NT,
                            preferred_element_type=F32) * scale
        p = jnp.exp(s - jnp.max(s, axis=1, keepdims=True))
        o = jnp.dot(p.astype(BF16), v[:, sl].astype(BF16), preferred_element_type=F32)
        o_ref[0, :, sl] = o / jnp.sum(p, axis=1, keepdims=True)


def _mem_attn(q_src, qcol, k_src, kcol, v_src, vcol, *, tq):
    b, t, _ = q_src.shape
    n = k_src.shape[1]
    return pl.pallas_call(
        _mem_attn_kernel, grid=(b, t // tq),
        in_specs=[pl.BlockSpec((1, tq, GW), lambda bi, i: (bi, i, qcol)),
                  pl.BlockSpec((1, n, GW), lambda bi, i: (bi, 0, kcol)),
                  pl.BlockSpec((1, n, GW), lambda bi, i: (bi, 0, vcol))],
        out_specs=pl.BlockSpec((1, tq, GW), lambda bi, i: (bi, i, 0)),
        out_shape=jax.ShapeDtypeStruct((b, t, GW), F32), compiler_params=_cparams(2), name="mem_attn",
    )(q_src, k_src, v_src)


def _mem_attn_t_kernel(q_ref, kt_ref, vt_ref, o_ref):
    q, kt, vt = q_ref[0], kt_ref[0], vt_ref[0]
    scale = HD ** -0.5
    for h in range(HEADS):
        sl = slice(h * HD, (h + 1) * HD)
        s = jnp.dot(q[:, sl].astype(BF16), kt[sl, :].astype(BF16), preferred_element_type=F32) * scale
        p = jnp.exp(s - jnp.max(s, axis=1, keepdims=True))
        o = lax.dot_general(p.astype(BF16), vt[sl, :].astype(BF16), _NT, preferred_element_type=F32)
        o_ref[0, :, sl] = o / jnp.sum(p, axis=1, keepdims=True)


def _mem_attn_t(layer, q_src, qcol, k_t, v_t):
    b, t, _ = q_src.shape
    n = k_t.shape[-1]
    cache = pl.BlockSpec((None, 1, GW, n), lambda bi: (layer, bi, 0, 0))
    return pl.pallas_call(
        _mem_attn_t_kernel, grid=(b,),
        in_specs=[pl.BlockSpec((1, t, GW), lambda bi: (bi, 0, qcol)), cache, cache],
        out_specs=pl.BlockSpec((1, t, GW), lambda bi: (bi, 0, 0)),
        out_shape=jax.ShapeDtypeStruct((b, t, GW), F32), compiler_params=_cparams(1), name="mem_attn_t",
    )(q_src, k_t, v_t)


def _out_proj_kernel(x_ref, ga_ref, gb_ref, gc_ref, gm_ref, oa_ref, ob_ref, oc_ref, om_ref,
                     w_ref, gpost_ref, y_ref):
    acc = None
    for idx, (o_r, g_r) in enumerate(((oa_ref, ga_ref), (ob_ref, gb_ref), (oc_ref, gc_ref), (om_ref, gm_ref))):
        g = g_r[...]
        part = (o_r[...] * (g * jax.nn.sigmoid(g))).astype(BF16)
        d = jnp.dot(part, w_ref[idx * GW:(idx + 1) * GW, :], preferred_element_type=F32)
        acc = d if acc is None else acc + d
    yn = acc * lax.rsqrt(jnp.mean(acc * acc, axis=-1, keepdims=True) + EPS) * gpost_ref[...]
    y_ref[...] = x_ref[...] + yn


def _out_proj(x2d, um, oa, ob, oc, om, w_out, g_post, *, tm):
    r, d = x2d.shape
    gate = lambda col: pl.BlockSpec((tm, GW), lambda i: (i, col))
    grp = pl.BlockSpec((tm, GW), lambda i: (i, 0))
    return pl.pallas_call(
        _out_proj_kernel, grid=(r // tm,),
        in_specs=[pl.BlockSpec((tm, d), lambda i: (i, 0)), gate(2), gate(4), gate(5), gate(7),
                  grp, grp, grp, grp,
                  pl.BlockSpec((4 * GW, d), lambda i: (0, 0)), pl.BlockSpec((1, d), lambda i: (0, 0))],
        out_specs=pl.BlockSpec((tm, d), lambda i: (i, 0)),
        out_shape=jax.ShapeDtypeStruct((r, d), F32), compiler_params=_cparams(1), name="out_proj",
    )(x2d, um, um, um, um, oa, ob, oc, om, w_out, g_post)


def _paged_fetch(pt_ref, layer, caches, bufs, sem, n_pages_step):
    s, g = pl.program_id(0), pl.program_id(1)
    ns, ng = pl.num_programs(0), pl.num_programs(1)
    n = s * ng + g
    slot = lax.rem(n, 2)

    def copy(a, page, slot_, j):
        return pltpu.make_async_copy(caches[a].at[layer, page], bufs[a].at[slot_, j], sem.at[slot_, a])

    def start(seq, grp, slot_):
        def body(j, carry):
            page = pt_ref[seq, grp * n_pages_step + j]
            for a in range(len(caches)):
                copy(a, page, slot_, j).start()
            return carry
        lax.fori_loop(0, n_pages_step, body, 0)

    @pl.when(n == 0)
    def _():
        start(0, 0, 0)

    @pl.when(n + 1 < ns * ng)
    def _():
        wraps = g + 1 == ng
        start(jnp.where(wraps, s + 1, s), jnp.where(wraps, 0, g + 1), 1 - slot)

    def wait_body(j, carry):
        for a in range(len(caches)):
            copy(a, 0, slot, j).wait()
        return carry
    lax.fori_loop(0, n_pages_step, wait_body, 0)
    return slot


def _sample_idx_kernel(pt_ref, qi_ref, wi_ref, kin_ref, cache_ref, scp_ref, scn_ref, kbuf, sem,
                       *, layer, n_pages_step):
    slot = _paged_fetch(pt_ref, layer, (cache_ref,), (kbuf,), sem, n_pages_step)
    scale = HD ** -0.5
    t = qi_ref.shape[1]
    qi4 = _split_bf16(_split_heads_rows(qi_ref[0], F32))
    wi = wi_ref[0]
    w_rows = jnp.concatenate([wi[:, h:h + 1] for h in range(HEADS)], axis=0)

    def score(s):
        r = jnp.maximum(s * scale, 0.0) * w_rows
        out = r[0:t]
        for h in range(1, HEADS):
            out = out + r[h * t:(h + 1) * t]
        return out

    kp = _split_bf16(jnp.concatenate([kbuf[slot, j] for j in range(n_pages_step)], axis=1))
    q3, k3 = _stack3(qi4, kp, 1, 0)
    scp_ref[0] = score(jnp.dot(q3, k3, preferred_element_type=F32))

    @pl.when(pl.program_id(1) == pl.num_programs(1) - 1)
    def _():
        kn = _split_bf16(jnp.concatenate([kin_ref[0], jnp.zeros((CK - t, HD), F32)], axis=0))
        sn = score(_dot3(qi4, kn, _NT))
        col = lax.broadcasted_iota(I32, (t, CK), 1)
        row = lax.broadcasted_iota(I32, (t, CK), 0)
        scn_ref[0] = jnp.where(col <= row, sn, -jnp.inf)


def _page_buffer(cache, n):
    return pltpu.VMEM((2, n) + tuple(cache.shape[2:]), cache.dtype)


def _sample_idx(layer, page_table, um_s, awi_s, aki_s, cache_idx_t):
    ns, t, _ = um_s.shape
    n_pages = page_table.shape[1]
    page = cache_idx_t.shape[3]
    p = min(IDX_PAGES_PER_STEP, n_pages)
    kern = functools.partial(_sample_idx_kernel, layer=layer, n_pages_step=p)
    grid_spec = pltpu.PrefetchScalarGridSpec(
        num_scalar_prefetch=1, grid=(ns, n_pages // p),
        in_specs=[pl.BlockSpec((1, t, GW), lambda s, g, pt: (s, 0, 1)),
                  pl.BlockSpec((1, t, 128), lambda s, g, pt: (s, 0, 0)),
                  pl.BlockSpec((1, t, HD), lambda s, g, pt: (s, 0, 0)),
                  pl.BlockSpec(memory_space=pl.ANY)],
        out_specs=(pl.BlockSpec((1, t, p * page), lambda s, g, pt: (s, 0, g)),
                   pl.BlockSpec((1, t, CK), lambda s, g, pt: (s, 0, 0))),
        scratch_shapes=[_page_buffer(cache_idx_t, p), pltpu.SemaphoreType.DMA((2, 1))])
    return pl.pallas_call(
        kern, grid_spec=grid_spec,
        out_shape=(jax.ShapeDtypeStruct((ns, t, n_pages * page), F32),
                   jax.ShapeDtypeStruct((ns, t, CK), F32)),
        compiler_params=_cparams(2), name="sample_idx",
    )(page_table, um_s, awi_s, aki_s, cache_idx_t)


def _topk_bias_kernel(scp_ref, scn_ref, tri_ref, bp_ref, bn_ref, sc_ref, *, topk):
    nchp = scp_ref.shape[1] // CK
    nch = nchp + 1
    for c in range(nchp):
        sc_ref[c] = scp_ref[:, c * CK:(c + 1) * CK]
    sc_ref[nchp] = scn_ref[...]

    t, t_next, need = _kth_largest(sc_ref, nch, topk)
    carry = jnp.zeros((sc_ref.shape[1], 1), F32)
    for c in range(nch):
        sel, carry = _select_chunk(sc_ref[c], t, t_next, need, carry, tri_ref[...])
        bias = jnp.where(sel, 0.0, MASK)
        if c < nchp:
            bp_ref[:, c * CK:(c + 1) * CK] = bias
        else:
            bn_ref[...] = bias


def _topk_bias(sc_past, sc_new, tri, *, topk, rb):
    nr, past = sc_past.shape
    kern = functools.partial(_topk_bias_kernel, topk=topk)
    return pl.pallas_call(
        kern, grid=(nr // rb,),
        in_specs=[pl.BlockSpec((rb, past), lambda i: (i, 0)), pl.BlockSpec((rb, CK), lambda i: (i, 0)),
                  pl.BlockSpec((CK, CK), lambda i: (0, 0))],
        out_specs=(pl.BlockSpec((rb, past), lambda i: (i, 0)), pl.BlockSpec((rb, CK), lambda i: (i, 0))),
        out_shape=(jax.ShapeDtypeStruct((nr, past), F32), jax.ShapeDtypeStruct((nr, CK), F32)),
        scratch_shapes=[pltpu.VMEM((past // CK + 1, rb, CK), F32)],
        compiler_params=_cparams(1), name="topk_bias",
    )(sc_past, sc_new, tri)


def _sample_attn_kernel(pt_ref, q_ref, qcat_ref, bp_ref, bn_ref, kn_ref, vn_ref, kcn_ref, et_ref, wuv_ref,
                        ck_ref, cv_ref, cc_ref, cr_ref, oa_ref, oc_ref,
                        kbuf, vbuf, cbuf, rbuf, sem, m1, l1, acc1, m2, l2, acc2,
                        *, layer, n_pages_step, kv_rank, scale_c):
    p = n_pages_step
    slot = _paged_fetch(pt_ref, layer, (ck_ref, cv_ref, cc_ref, cr_ref), (kbuf, vbuf, cbuf, rbuf), sem, p)
    g = pl.program_id(1)
    t = q_ref.shape[1]
    scale_a = HD ** -0.5

    @pl.when(g == 0)
    def _():
        _softmax_init(m1, l1, acc1)
        _softmax_init(m2, l2, acc2)

    q4 = _split_heads_rows(q_ref[0])
    qc = qcat_ref[:, 0].reshape(HEADS * t, qcat_ref.shape[-1])
    qcb = qc.astype(BF16)
    qlat = qcb[:, :kv_rank]
    qrope = jnp.dot(qcb[:, kv_rank:], et_ref[...], preferred_element_type=F32).astype(BF16)

    cat = lambda buf, axis: jnp.concatenate([buf[slot, j] for j in range(p)], axis=axis).astype(BF16)
    kp, vp, cp, rp = cat(kbuf, 1), cat(vbuf, 1), cat(cbuf, 0), cat(rbuf, 1)
    s1 = jnp.dot(q4, kp, preferred_element_type=F32) * scale_a + _tile_rows(bp_ref[0], HEADS)
    _softmax_update(m1, l1, acc1, s1, vp, v_is_transposed=True)
    s2 = (lax.dot_general(qlat, cp, _NT, preferred_element_type=F32)
          + jnp.dot(qrope, rp, preferred_element_type=F32)) * scale_c
    _softmax_update(m2, l2, acc2, s2, cp)

    @pl.when(g == pl.num_programs(1) - 1)
    def _():
        zpad = lambda a: jnp.concatenate([a, jnp.zeros((LANES - t, a.shape[1]), F32)], axis=0).astype(BF16)
        kn, vn, kcn = zpad(kn_ref[0]), zpad(vn_ref[0]), zpad(kcn_ref[0])
        s1n = (lax.dot_general(q4, kn, _NT, preferred_element_type=F32) * scale_a
               + _tile_rows(bn_ref[0][:, :LANES], HEADS))
        _softmax_update(m1, l1, acc1, s1n, vn)
        col = lax.broadcasted_iota(I32, (t, LANES), 1)
        row = lax.broadcasted_iota(I32, (t, LANES), 0)
        causal = jnp.where(col <= row, 0.0, MASK)
        s2n = lax.dot_general(qcb, kcn, _NT, preferred_element_type=F32) * scale_c + _tile_rows(causal, HEADS)
        _softmax_update(m2, l2, acc2, s2n, kcn[:, :kv_rank])
        oa = acc1[...] / l1[...]
        oc = acc2[...] / l2[...]
        for h in range(HEADS):
            oa_ref[0, :, h * HD:(h + 1) * HD] = oa[h * t:(h + 1) * t]
            oc_ref[0, :, h * HD:(h + 1) * HD] = jnp.dot(oc[h * t:(h + 1) * t].astype(BF16), wuv_ref[h],
                                                        preferred_element_type=F32)


def _sample_attn(layer, page_table, um_s, qcat_s, bias_past, bias_new, ak_s, av_s, kcat_s, et, wuv,
                 cache_k_t, cache_v_t, cache_c, cache_r_t, *, scale_c):
    ns, t, _ = um_s.shape
    n_pages = page_table.shape[1]
    page = cache_c.shape[2]
    kv_rank = cache_c.shape[3]
    rope = cache_r_t.shape[2]
    w = qcat_s.shape[-1]
    p = min(ATTN_PAGES_PER_STEP, n_pages)
    kern = functools.partial(_sample_attn_kernel, layer=layer, n_pages_step=p, kv_rank=kv_rank,
                             scale_c=scale_c)
    hbm = pl.BlockSpec(memory_space=pl.ANY)
    seq = lambda width: pl.BlockSpec((1, t, width), lambda s, g, pt: (s, 0, 0))
    grid_spec = pltpu.PrefetchScalarGridSpec(
        num_scalar_prefetch=1, grid=(ns, n_pages // p),
        in_specs=[seq(GW),
                  pl.BlockSpec((HEADS, 1, t, w), lambda s, g, pt: (0, s, 0, 0)),
                  pl.BlockSpec((1, t, p * page), lambda s, g, pt: (s, 0, g)),
                  seq(CK), seq(HD), seq(HD), seq(w),
                  pl.BlockSpec((128, rope), lambda s, g, pt: (0, 0)),
                  pl.BlockSpec((HEADS, kv_rank, HD), lambda s, g, pt: (0, 0, 0)),
                  hbm, hbm, hbm, hbm],
        out_specs=(seq(GW), seq(GW)),
        scratch_shapes=[_page_buffer(cache_k_t, p), _page_buffer(cache_v_t, p), _page_buffer(cache_c, p),
                        _page_buffer(cache_r_t, p), pltpu.SemaphoreType.DMA((2, 4)),
                        pltpu.VMEM((HEADS * t, 1), F32), pltpu.VMEM((HEADS * t, 1), F32),
                        pltpu.VMEM((HEADS * t, HD), F32), pltpu.VMEM((HEADS * t, 1), F32),
                        pltpu.VMEM((HEADS * t, 1), F32), pltpu.VMEM((HEADS * t, kv_rank), F32)])
    return pl.pallas_call(
        kern, grid_spec=grid_spec,
        out_shape=(jax.ShapeDtypeStruct((ns, t, GW), F32), jax.ShapeDtypeStruct((ns, t, GW), F32)),
        compiler_params=_cparams(2), name="sample_attn",
    )(page_table, um_s, qcat_s, bias_past, bias_new, ak_s, av_s, kcat_s, et, wuv,
      cache_k_t, cache_v_t, cache_c, cache_r_t)


def _prep_layer(l, g_pre, g_post, w_in, w_out, conv_w, conv_b, w_ra, b_ra, w_ri, b_ri, lam,
                g_q, w_uq, g_kv, w_uk, w_uv, w_mk, w_mv):
    q_rank = g_q.shape[-1]
    kv_rank = g_kv.shape[-1]
    nope = w_uk.shape[-1]
    rope = w_uq.shape[-1] // HEADS - nope
    half = rope // 2
    assert GW == w_in.shape[1] // 4 and rope == 32 and nope == HD and q_rank <= 256 and kv_rank == 128
    widths = (GW, HD, HD, GW, HD, HEADS, GW, GW, GW, q_rank, kv_rank, rope, GW, GW, GW)
    assert sum(widths) == w_in.shape[-1]
    offs = np.concatenate([[0], np.cumsum(widths)])
    w = w_in[l]
    col = lambda i: w[:, offs[i]:offs[i + 1]]
    padc = lambda a, n: jnp.pad(a, ((0, 0), (0, n - a.shape[1])))
    aq, ak, av, aqi, aki, awi, ag, bx, bg, cdq, cdkv, ckr, cg, mq, mg = [col(i) for i in range(15)]
    kr_dup = jnp.concatenate([ckr[:, :half]] * HEADS + [ckr[:, half:]] * HEADS, axis=1)
    w_all = jnp.concatenate([aq, aqi, ag, bx, bg, cg, mq, mg, padc(ak, 128), padc(av, 128), padc(aki, 128),
                             padc(awi, 128), padc(cdq, 256), cdkv, kr_dup], axis=1)
    assert w_all.shape[1] == _NP
    w_idx = jnp.concatenate([aqi, padc(aki, 128), padc(awi, 128)], axis=1)
    w_idx_hi = w_idx.astype(BF16)
    w_idx_lo = (w_idx - w_idx_hi.astype(F32)).astype(BF16)

    wq = w_uq[l].reshape(q_rank, HEADS, nope + rope)
    wq2 = jnp.concatenate([wq[:, :, :nope].reshape(q_rank, HEADS * nope),
                           wq[:, :, nope:nope + half].reshape(q_rank, HEADS * half),
                           wq[:, :, nope + half:].reshape(q_rank, HEADS * half)], axis=1)
    wq2 = jnp.pad(wq2, ((0, 256 - q_rank), (0, 0)))
    wuk_bd = jnp.zeros((HEADS * nope, HEADS * kv_rank), F32)
    for h in range(HEADS):
        wuk_bd = wuk_bd.at[h * nope:(h + 1) * nope, h * kv_rank:(h + 1) * kv_rank].set(w_uk[l][:, h, :].T)

    def block_diag(wb):
        nb, bd, _ = wb.shape
        out = jnp.zeros((nb * bd, nb * bd), F32)
        for n in range(nb):
            out = out.at[n * bd:(n + 1) * bd, n * bd:(n + 1) * bd].set(wb[n])
        return out

    assert half == ROPE_HALF
    lane = np.arange(128)
    head_mask = np.zeros((8, 128), np.float32)
    for h in range(HEADS):
        head_mask[h] = ((lane % 64) // half == h)
    return {
        "q_rank": q_rank, "kv_rank": kv_rank,
        "g_pre": g_pre[l][None], "g_post": g_post[l][None],
        "w_in": w_all.astype(BF16), "w_idx_hi": w_idx_hi, "w_idx_lo": w_idx_lo,
        "g_q": jnp.pad(g_q[l], (0, 256 - q_rank))[None], "w_q2": wq2.astype(BF16),
        "w_uk_bd": wuk_bd.astype(BF16), "g_kv": g_kv[l][None],
        "head_mask": jnp.asarray(head_mask),
        "w_out": w_out[l].astype(BF16),
        "conv_w": conv_w[l], "conv_b": conv_b[l][None],
        "w_ra_bd": block_diag(w_ra[l]).astype(BF16), "b_ra": b_ra[l][None],
        "w_ri_bd": block_diag(w_ri[l]).astype(BF16), "b_ri": b_ri[l][None], "lam": lam[l][None],
        "w_uv": jnp.transpose(w_uv[l], (1, 0, 2)).astype(BF16),
        "w_mkv": jnp.concatenate([w_mk[l], w_mv[l]], axis=1).astype(BF16),
        "scale_c": float((nope + rope) ** -0.5),
    }


def kernel(x_prompt, x_sample, mem_prompt, cache_a_k, cache_a_v, cache_a_idx, cache_c_lat, cache_c_kr,
           state_b_h, state_b_conv, cache_mem_k, cache_mem_v, page_table,
           g_pre, g_post, w_in, w_out, conv_w, conv_b, w_ra, b_ra, w_ri, b_ri, lam,
           g_q, w_uq, g_kv, w_uk, w_uv, w_mk, w_mv):
    b, s, d = x_prompt.shape
    ns, t, _ = x_sample.shape
    depth = w_in.shape[0]
    n_mem = mem_prompt.shape[1]
    past = page_table.shape[1] * cache_a_k.shape[2]
    rope = cache_c_kr.shape[-1]
    n_pages = page_table.shape[1]
    assert s % CK == 0 and past % CK == 0 and t == 8
    assert n_pages % min(IDX_PAGES_PER_STEP, n_pages) == 0 and n_pages % min(ATTN_PAGES_PER_STEP, n_pages) == 0

    tri = jnp.asarray(np.triu(np.ones((CK, CK), np.float32), k=1), BF16)
    et_np = np.zeros((128, rope), np.float32)
    for lane in range(128):
        et_np[lane, (lane // 64) * (rope // 2) + lane % (rope // 2)] = 1.0
    et = jnp.asarray(et_np, BF16)

    cache_idx_t = jnp.swapaxes(cache_a_idx, 2, 3)
    cache_k_t = jnp.swapaxes(cache_a_k, 2, 3)
    cache_v_t = jnp.swapaxes(cache_a_v, 2, 3)
    cache_r_t = jnp.swapaxes(cache_c_kr, 2, 3)
    mem_k_t = jnp.transpose(cache_mem_k, (0, 1, 3, 4, 2)).reshape(depth, ns, GW, n_mem)
    mem_v_t = jnp.transpose(cache_mem_v, (0, 1, 3, 4, 2)).reshape(depth, ns, GW, n_mem)

    xp = x_prompt.reshape(b * s, d)
    xs = x_sample.reshape(ns * t, d)
    mem2d = mem_prompt.reshape(b * n_mem, d)
    p_states = [[] for _ in range(9)]
    s_states = [[] for _ in range(7)]
    for l in range(depth):
        wts = _prep_layer(l, g_pre, g_post, w_in, w_out, conv_w, conv_b, w_ra, b_ra, w_ri, b_ri, lam,
                          g_q, w_uq, g_kv, w_uk, w_uv, w_mk, w_mv)
        kv_rank = wts["kv_rank"]
        um, ak, av, aki, awi, c, kr, qcat, kcat = _in_proj(xp, wts, tm=512, pos_base=0, pos_mod=s)
        um3 = um.reshape(b, s, _UM)
        r3 = lambda a: a.reshape(b, s, a.shape[-1])
        oa = _dsa_prompt(um3, r3(awi), r3(aki), r3(ak), r3(av), tri, tq=128)
        oc = _mla_prompt(qcat.reshape(HEADS, b, s, kv_rank + 128), r3(kcat), wts["w_uv"], tq=128,
                         scale=wts["scale_c"])
        ob, bh = _rglru(um3, 3, jnp.zeros((b, GW), F32), jnp.zeros((b, 3, GW), F32), wts, tc=512)
        mkv = _matmul(mem2d, wts["w_mkv"], tm=256).reshape(b, n_mem, 2 * GW)
        om = _mem_attn(um3, 6, mkv, 0, mkv, 1, tq=256)
        flat = lambda a: a.reshape(b * s, GW)
        xp_new = _out_proj(xp, um, flat(oa), flat(ob), flat(oc), flat(om), wts["w_out"], wts["g_post"], tm=512)
        st_p = (r3(ak), r3(av), r3(aki), r3(c), r3(kr), bh.reshape(b, GW), um3[:, s - 3:, 3 * GW:4 * GW],
                mkv[:, :, :GW].reshape(b, n_mem, HEADS, HD), mkv[:, :, GW:].reshape(b, n_mem, HEADS, HD))
        um_s, ak_s, av_s, aki_s, awi_s, c_s, kr_s, qcat_s, kcat_s = _in_proj(
            xs, wts, tm=min(512, ns * t), pos_base=past, pos_mod=t)
        um_s3 = um_s.reshape(ns, t, _UM)
        q3 = lambda a: a.reshape(ns, t, a.shape[-1])
        sc_past, sc_new = _sample_idx(l, page_table, um_s3, q3(awi_s), q3(aki_s), cache_idx_t)
        topk = min(TOPK_MAX, (past + t) // 4)
        bias_past, bias_new = _topk_bias(sc_past.reshape(ns * t, past), sc_new.reshape(ns * t, CK), tri,
                                         topk=topk, rb=min(64, ns * t))
        oa_s, oc_s = _sample_attn(l, page_table, um_s3, qcat_s.reshape(HEADS, ns, t, kv_rank + 128),
                                  bias_past.reshape(ns, t, past), bias_new.reshape(ns, t, CK),
                                  q3(ak_s), q3(av_s), q3(kcat_s), et, wts["w_uv"],
                                  cache_k_t, cache_v_t, cache_c_lat, cache_r_t, scale_c=wts["scale_c"])
        ob_s, bh_s = _rglru(um_s3, 3, state_b_h[l], state_b_conv[l], wts, tc=t)
        om_s = _mem_attn_t(l, um_s3, 6, mem_k_t, mem_v_t)
        flat_s = lambda a: a.reshape(ns * t, GW)
        xs_new = _out_proj(xs, um_s, flat_s(oa_s), flat_s(ob_s), flat_s(oc_s), flat_s(om_s),
                           wts["w_out"], wts["g_post"], tm=min(512, ns * t))
        st_s = (q3(ak_s), q3(av_s), q3(aki_s), q3(c_s), q3(kr_s), bh_s.reshape(ns, GW),
                um_s3[:, t - 3:, 3 * GW:4 * GW])
        xp, xs = xp_new, xs_new
        for lst, v in zip(p_states, st_p):
            lst.append(v)
        for lst, v in zip(s_states, st_s):
            lst.append(v)
    outs_p = [jnp.stack(v) for v in p_states]
    outs_s = [jnp.stack(v) for v in s_states]
    return (xp.reshape(b, s, d), xs.reshape(ns, t, d), *outs_p, *outs_s)
```
